```python
import jax, jax.numpy as jnp
from jax import lax
import numpy as np

D_MODEL = 1024
BATCH = 16
SEQ = 2048
DEPTH = 2

ROPE_THETA = 10000.0
NORM_EPS = 1e-6
HEAD_NORM_EPS = 1e-5
RET_HEADS = 4
RET_HEAD_DIM = 256
RET_WIDTH = RET_HEADS * RET_HEAD_DIM
RET_CHUNK = 128
ATT_GROUPS = ((128, 1), (512, 4), (2048, 16))
ATT_N_GROUPS = 3
ATT_HEADS_PER_GROUP = 8
ATT_HEAD_DIM = 128
ATT_QKV_WIDTH = ATT_N_GROUPS * ATT_HEADS_PER_GROUP * ATT_HEAD_DIM
ATT_WIDTH = ATT_HEADS_PER_GROUP * ATT_HEAD_DIM
ATT_BLOCK = 128
CONV_WIDTH = 1024
CONV_TAPS = 31
N_BRANCHES = 3
IN_SPLITS = (RET_WIDTH, RET_WIDTH, RET_WIDTH, RET_WIDTH,
             ATT_QKV_WIDTH, ATT_QKV_WIDTH, ATT_QKV_WIDTH, ATT_WIDTH,
             2 * CONV_WIDTH, CONV_WIDTH, N_BRANCHES * D_MODEL)
IN_WIDTH = 4 * RET_WIDTH + 3 * ATT_QKV_WIDTH + ATT_WIDTH + 3 * CONV_WIDTH + N_BRANCHES * D_MODEL

kernel_name = "hybrid_retention_dilated_conv_gated"


def rms_norm(x, w):
    xf = x.astype(jnp.float32)
    y = xf * lax.rsqrt(jnp.mean(xf * xf, axis=-1, keepdims=True) + NORM_EPS)
    return (y * w).astype(x.dtype)


def layer_norm(x, w, b):
    xf = x.astype(jnp.float32)
    mu = jnp.mean(xf, axis=-1, keepdims=True)
    var = jnp.mean(jnp.square(xf - mu), axis=-1, keepdims=True)
    return ((xf - mu) * lax.rsqrt(var + HEAD_NORM_EPS) * w + b).astype(x.dtype)


def head_norm(x):
    xf = x.astype(jnp.float32)
    mu = jnp.mean(xf, axis=-1, keepdims=True)
    var = jnp.mean(jnp.square(xf - mu), axis=-1, keepdims=True)
    return ((xf - mu) * lax.rsqrt(var + HEAD_NORM_EPS)).astype(x.dtype)


def rope(x, pos):
    hd = x.shape[-1]
    inv = ROPE_THETA ** (-jnp.arange(0, hd, 2, dtype=jnp.float32) / hd)
    ang = pos.astype(jnp.float32)[:, None] * inv[None, :]
    shape = (1, ang.shape[0]) + (1,) * (x.ndim - 3) + (hd // 2,)
    cos = jnp.cos(ang).reshape(shape)
    sin = jnp.sin(ang).reshape(shape)
    x1, x2 = jnp.split(x.astype(jnp.float32), 2, axis=-1)
    return jnp.concatenate([x1 * cos - x2 * sin, x2 * cos + x1 * sin], axis=-1).astype(x.dtype)


def retention(q, k, v):
    B, S, H, hd = q.shape
    C = RET_CHUNK
    N = S // C
    dt = q.dtype
    lg = jnp.log(1.0 - 2.0 ** (-5.0 - jnp.arange(H, dtype=jnp.float32)))
    idx = jnp.arange(C, dtype=jnp.float32)
    diff = idx[:, None] - idx[None, :]
    intra_decay = jnp.where(diff[None] >= 0,
                            jnp.exp(jnp.maximum(diff, 0.0)[None] * lg[:, None, None]), 0.0)
    q_decay = jnp.exp((idx + 1.0)[:, None] * lg[None, :]).astype(dt)
    k_decay = jnp.exp((C - 1.0 - idx)[:, None] * lg[None, :]).astype(dt)
    chunk_decay = jnp.exp(C * lg).astype(dt)
    qc = q.reshape(B, N, C, H, hd)
    kc = k.reshape(B, N, C, H, hd)
    vc = v.reshape(B, N, C, H, hd)
    scores = jnp.einsum('bnihd,bnjhd->bnhij', qc, kc) * intra_decay.astype(dt)
    intra = jnp.einsum('bnhij,bnjhe->bnihe', scores, vc)

    def step(state, inp):
        qn, kn, vn = inp
        inter = jnp.einsum('bihd,bhde->bihe', qn, state) * q_decay[None, :, :, None]
        state = (state * chunk_decay[None, :, None, None]
                 + jnp.einsum('bjhd,bjhe->bhde', kn * k_decay[None, :, :, None], vn))
        return state, inter

    init = jnp.zeros((B, H, hd, hd), dt)
    _, inter = lax.scan(step, init, (jnp.moveaxis(qc, 1, 0), jnp.moveaxis(kc, 1, 0), jnp.moveaxis(vc, 1, 0)))
    inter = jnp.moveaxis(inter, 0, 1)
    return (intra + inter).reshape(B, S, H, hd)


def dilated_group(q, k, v, window, dilation):
    B, S, Hg, hd = q.shape
    blk = ATT_BLOCK
    L = S // dilation
    n_blk = -(-L // blk)
    Lp = n_blk * blk
    span = window // dilation
    qs = q.reshape(B, L, dilation, Hg, hd)
    ks = k.reshape(B, L, dilation, Hg, hd)
    vs = v.reshape(B, L, dilation, Hg, hd)
    qs = jnp.pad(qs, ((0, 0), (0, Lp - L), (0, 0), (0, 0), (0, 0)))
    ks = jnp.pad(ks, ((0, 0), (blk, Lp - L), (0, 0), (0, 0), (0, 0)))
    vs = jnp.pad(vs, ((0, 0), (blk, Lp - L), (0, 0), (0, 0), (0, 0)))
    qb = qs.reshape(B, n_blk, blk, dilation, Hg, hd)
    kb = ks.reshape(B, n_blk + 1, blk, dilation, Hg, hd)
    vb = vs.reshape(B, n_blk + 1, blk, dilation, Hg, hd)
    kk = jnp.concatenate([kb[:, :-1], kb[:, 1:]], axis=2)
    vv = jnp.concatenate([vb[:, :-1], vb[:, 1:]], axis=2)
    s = jnp.einsum('bnidhe,bnjdhe->bnidhj', qb, kk).astype(jnp.float32) * (hd ** -0.5)
    i = jnp.arange(blk)[:, None]
    j = jnp.arange(2 * blk)[None, :]
    dist = i - j + blk
    key_l = jnp.arange(n_blk)[:, None, None] * blk + j[None] - blk
    valid = (dist[None] >= 0) & (dist[None] <= span) & (key_l >= 0)
    s = jnp.where(valid[None, :, :, None, None, :], s, -jnp.inf)
    m = jnp.max(s, axis=-1, keepdims=True)
    p = jnp.exp(s - m)
    den = jnp.sum(p, axis=-1)
    o = jnp.einsum('bnidhj,bnjdhe->bnidhe', p, vv.astype(jnp.float32)) / den[..., None]
    lse = m[..., 0] + jnp.log(den)
    o = o.reshape(B, Lp, dilation, Hg, hd)[:, :L].reshape(B, S, Hg, hd)
    lse = lse.reshape(B, Lp, dilation, Hg)[:, :L].reshape(B, S, Hg)
    return o, lse


def hybrid_layer(x, pos, norm_w, w_in, b_in, ret_norm_w, ret_w_o, att_w_o,
                 conv_dw_w, conv_dw_b, conv_norm_w, conv_norm_b, conv_w_o, w_out):
    B, S, D = x.shape
    h = rms_norm(x, norm_w)
    z = h @ w_in + b_in
    split_points = [int(c) for c in np.cumsum(IN_SPLITS)[:-1]]
    rq, rk, rv, rg, aq, ak, av, ag, cu, cg, mg = jnp.split(z, split_points, axis=-1)

    rshp = (B, S, RET_HEADS, RET_HEAD_DIM)
    rq = rope(rq.reshape(rshp), pos)
    rk = rope(rk.reshape(rshp), pos) * (RET_HEAD_DIM ** -0.5)
    r = retention(rq, rk, rv.reshape(rshp))
    r = head_norm(r).reshape(B, S, RET_WIDTH) * ret_norm_w
    y_ret = (r * jax.nn.silu(rg)) @ ret_w_o

    ashp = (B, S, ATT_N_GROUPS, ATT_HEADS_PER_GROUP, ATT_HEAD_DIM)
    aq = rope(aq.reshape(ashp), pos)
    ak = rope(ak.reshape(ashp), pos)
    av = av.reshape(ashp)
    outs, lses = [], []
    for g, (window, dil) in enumerate(ATT_GROUPS):
        o, l = dilated_group(aq[:, :, g], ak[:, :, g], av[:, :, g], window, dil)
        outs.append(o)
        lses.append(l)
    wts = jax.nn.softmax(jnp.stack(lses, axis=0), axis=0)
    a = jnp.sum(wts[..., None] * jnp.stack(outs, axis=0), axis=0)
    a = a.astype(x.dtype).reshape(B, S, ATT_WIDTH)
    y_att = (a * jax.nn.silu(ag)) @ att_w_o

    c_lin, c_gate = jnp.split(cu, 2, axis=-1)
    c = c_lin * jax.nn.sigmoid(c_gate)
    c = lax.conv_general_dilated(c, conv_dw_w[:, None, :], window_strides=(1,),
                                 padding=[(CONV_TAPS - 1, 0)],
                                 dimension_numbers=('NWC', 'WIO', 'NWC'),
                                 feature_group_count=CONV_WIDTH) + conv_dw_b
    c = jax.nn.silu(layer_norm(c, conv_norm_w, conv_norm_b))
    y_conv = (c * jax.nn.silu(cg)) @ conv_w_o

    gates = jax.nn.sigmoid(mg).reshape(B, S, N_BRANCHES, D)
    merged = gates[:, :, 0] * y_ret + gates[:, :, 1] * y_att + gates[:, :, 2] * y_conv
    return x + merged @ w_out


def setup_inputs(seed: int = 0) -> dict:
    key = jax.random.key(seed)
    ks = jax.random.split(key, 14)
    nrm = jax.random.normal
    f32 = jnp.float32
    return {
        "x": nrm(ks[0], (BATCH, SEQ, D_MODEL), f32),
        "norm_w": 1.0 + 0.02 * nrm(ks[1], (DEPTH, D_MODEL), f32),
        "w_in": nrm(ks[2], (DEPTH, D_MODEL, IN_WIDTH), f32) * D_MODEL ** -0.5,
        "b_in": 0.02 * nrm(ks[3], (DEPTH, IN_WIDTH), f32),
        "ret_norm_w": 1.0 + 0.02 * nrm(ks[4], (DEPTH, RET_WIDTH), f32),
        "ret_w_o": nrm(ks[5], (DEPTH, RET_WIDTH, D_MODEL), f32) * RET_WIDTH ** -0.5,
        "att_w_o": nrm(ks[6], (DEPTH, ATT_WIDTH, D_MODEL), f32) * ATT_WIDTH ** -0.5,
        "conv_dw_w": nrm(ks[7], (DEPTH, CONV_TAPS, CONV_WIDTH), f32) * CONV_TAPS ** -0.5,
        "conv_dw_b": 0.02 * nrm(ks[8], (DEPTH, CONV_WIDTH), f32),
        "conv_norm_w": 1.0 + 0.02 * nrm(ks[9], (DEPTH, CONV_WIDTH), f32),
        "conv_norm_b": 0.02 * nrm(ks[10], (DEPTH, CONV_WIDTH), f32),
        "conv_w_o": nrm(ks[11], (DEPTH, CONV_WIDTH, D_MODEL), f32) * CONV_WIDTH ** -0.5,
        "w_out": nrm(ks[12], (DEPTH, D_MODEL, D_MODEL), f32) * D_MODEL ** -0.5,
        "final_norm_w": 1.0 + 0.02 * nrm(ks[13], (D_MODEL,), f32),
    }


def reference(x, norm_w, w_in, b_in, ret_norm_w, ret_w_o, att_w_o, conv_dw_w, conv_dw_b,
              conv_norm_w, conv_norm_b, conv_w_o, w_out, final_norm_w):
    pos = jnp.arange(x.shape[1], dtype=jnp.int32)
    for l in range(DEPTH):
        x = hybrid_layer(x, pos, norm_w[l], w_in[l], b_in[l], ret_norm_w[l], ret_w_o[l], att_w_o[l],
                         conv_dw_w[l], conv_dw_b[l], conv_norm_w[l], conv_norm_b[l], conv_w_o[l], w_out[l])
    return rms_norm(x, final_norm_w)
```

```python
import functools

import numpy as np
import jax
import jax.numpy as jnp
from jax import lax
from jax.experimental import pallas as pl
from jax.experimental.pallas import tpu as pltpu

F32 = jnp.float32
BF16 = jnp.bfloat16

D_MODEL = 1024
SEQ = 2048
ROPE_THETA = 10000.0
NORM_EPS = 1e-6
HEAD_NORM_EPS = 1e-5
RET_HEADS = 4
RET_HEAD_DIM = 256
RET_CHUNK = 128
ATT_DILATIONS = (1, 4, 16)
ATT_HEADS_PER_GROUP = 8
ATT_HEAD_DIM = 128
ATT_BLOCK = 128
CONV_TAPS = 31
IN_WIDTH = 20480

LANES = 128
VMEM_LIMIT = 56 * 1024 * 1024

_COL_RQ, _COL_RK, _COL_RV, _COL_RG = 0, 1, 2, 3
_COL_AQ, _COL_AK, _COL_AV, _COL_AG = 4, 7, 10, 13
_COL_CLIN, _COL_CGATE, _COL_CG, _COL_MG = 14, 15, 16, 17

_EPI_ROPE256, _EPI_NONE, _EPI_SILU, _EPI_SIGMOID, _EPI_ROPE128 = 0, 1, 2, 3, 4

_TILE_PLAN = (
    (_COL_RQ, 0, _EPI_ROPE256, 0), (_COL_RK, 0, _EPI_ROPE256, 1),
    (_COL_RV, 0, _EPI_NONE, 0), (_COL_RG, 0, _EPI_SILU, 0),
    (_COL_AQ, 0, _EPI_ROPE128, 0), (_COL_AK, 0, _EPI_ROPE128, 0), (_COL_AV, 0, _EPI_NONE, 0),
    (_COL_AG, 0, _EPI_SILU, 0), (_COL_CLIN, 0, _EPI_NONE, 0), (_COL_CGATE, 0, _EPI_SIGMOID, 0),
    (_COL_CG, 0, _EPI_SILU, 0),
    (_COL_MG, 0, _EPI_SIGMOID, 0), (_COL_MG + 1, 0, _EPI_SIGMOID, 0), (_COL_MG + 2, 0, _EPI_SIGMOID, 0),
    (_COL_AQ + 1, 1, _EPI_ROPE128, 0), (_COL_AK + 1, 1, _EPI_ROPE128, 0), (_COL_AV + 1, 1, _EPI_NONE, 0),
    (_COL_AQ + 2, 2, _EPI_ROPE128, 0), (_COL_AK + 2, 2, _EPI_ROPE128, 0), (_COL_AV + 2, 2, _EPI_NONE, 0),
)


def _cparams(sem):
    return pltpu.CompilerParams(dimension_semantics=sem, vmem_limit_bytes=VMEM_LIMIT)


def _norm_perm_kernel(x_ref, w_ref, hs_ref, slab_ref):
    x = x_ref[...]
    h = x * lax.rsqrt(jnp.mean(x * x, axis=-1, keepdims=True) + NORM_EPS) * w_ref[...]
    hs_ref[0] = h.astype(BF16)
    n_slab = D_MODEL // LANES
    for c in range(n_slab):
        slab_ref[c] = h[:, c * LANES:(c + 1) * LANES]
    for li, d in enumerate(ATT_DILATIONS[1:], start=1):
        L = SEQ // d
        for r in range(d):
            for c in range(n_slab):
                hs_ref[li, r * L:(r + 1) * L, c * LANES:(c + 1) * LANES] = (
                    slab_ref[c, pl.ds(r, L, stride=d), :].astype(BF16))


def _norm_perm(xf, norm_w):
    T = xf.shape[0]
    return pl.pallas_call(
        _norm_perm_kernel,
        grid=(T // SEQ,),
        in_specs=[pl.BlockSpec((SEQ, D_MODEL), lambda b: (b, 0)),
                  pl.BlockSpec((1, D_MODEL), lambda b: (0, 0))],
        out_specs=pl.BlockSpec((3, SEQ, D_MODEL), lambda b: (0, b, 0)),
        out_shape=jax.ShapeDtypeStruct((3, T, D_MODEL), BF16),
        scratch_shapes=[pltpu.VMEM((D_MODEL // LANES, SEQ, LANES), F32)],
        compiler_params=_cparams(("arbitrary",)),
        name="norm_perm",
    )(xf, norm_w.reshape(1, D_MODEL))


_PROJ_TM = 1024
_PROJ_TN = 1024
_PROJ_NC = 256


def _in_proj_kernel(tbl_ref, hs_ref, w_ref, b_ref, t256_ref, t128_ref, o_ref):
    kind = tbl_ref[2, pl.program_id(1)]

    def chunk(nc):
        cols = slice(nc * _PROJ_NC, (nc + 1) * _PROJ_NC)
        z = jnp.dot(hs_ref[...], w_ref[:, cols], preferred_element_type=F32)
        return z + b_ref[:, cols]

    def emit(epilogue):
        for nc in range(_PROJ_TN // _PROJ_NC):
            cols = slice(nc * _PROJ_NC, (nc + 1) * _PROJ_NC)
            o_ref[:, cols] = epilogue(chunk(nc)).astype(BF16)

    def rope256(z):
        cos, sin = t256_ref[0], t256_ref[1]
        x1, x2 = z[:, :LANES], z[:, LANES:]
        return jnp.concatenate([x1 * cos - x2 * sin, x2 * cos + x1 * sin], axis=1)

    def rope128(z):
        cos_full, sin_signed = t128_ref[0], t128_ref[1]
        halves = []
        for hh in range(_PROJ_NC // LANES):
            x = z[:, hh * LANES:(hh + 1) * LANES]
            halves.append(x * cos_full + pltpu.roll(x, LANES // 2, 1) * sin_signed)
        return jnp.concatenate(halves, axis=1)

    epilogues = {
        _EPI_ROPE256: rope256,
        _EPI_NONE: lambda z: z,
        _EPI_SILU: lambda z: z * jax.nn.sigmoid(z),
        _EPI_SIGMOID: jax.nn.sigmoid,
        _EPI_ROPE128: rope128,
    }
    for k, fn in epilogues.items():
        pl.when(kind == k)(functools.partial(emit, fn))


def _in_proj(hs, w_bf16, b_in, t256, t128):
    T = hs.shape[1]
    tbl = jnp.asarray(np.array(_TILE_PLAN, dtype=np.int32).T)
    n_pos = SEQ // _PROJ_TM
    grid_spec = pltpu.PrefetchScalarGridSpec(
        num_scalar_prefetch=1,
        grid=(T // _PROJ_TM, IN_WIDTH // _PROJ_TN),
        in_specs=[
            pl.BlockSpec((None, _PROJ_TM, D_MODEL), lambda i, j, t: (t[1, j], i, 0)),
            pl.BlockSpec((D_MODEL, _PROJ_TN), lambda i, j, t: (0, t[0, j])),
            pl.BlockSpec((1, _PROJ_TN), lambda i, j, t: (0, t[0, j])),
            pl.BlockSpec((None, 2, _PROJ_TM, LANES), lambda i, j, t: (t[3, j], 0, i % n_pos, 0)),
            pl.BlockSpec((None, 2, _PROJ_TM, LANES), lambda i, j, t: (t[1, j], 0, i % n_pos, 0)),
        ],
        out_specs=pl.BlockSpec((_PROJ_TM, _PROJ_TN), lambda i, j, t: (i, t[0, j])),
    )
    return pl.pallas_call(
        _in_proj_kernel,
        grid_spec=grid_spec,
        out_shape=jax.ShapeDtypeStruct((T, IN_WIDTH), BF16),
        compiler_params=_cparams(("arbitrary", "arbitrary")),
        name="in_proj",
    )(tbl, hs, w_bf16, b_in.reshape(1, IN_WIDTH), t256, t128)


def _rope_tables():
    pos = jnp.arange(SEQ, dtype=jnp.int32)

    def cos_sin(hd, p):
        inv = ROPE_THETA ** (-jnp.arange(0, hd, 2, dtype=F32) / hd)
        ang = p.astype(F32)[:, None] * inv[None, :]
        return jnp.cos(ang), jnp.sin(ang)

    c, s = cos_sin(RET_HEAD_DIM, pos)
    k_scale = RET_HEAD_DIM ** -0.5
    t256 = jnp.stack([jnp.stack([c, s]), jnp.stack([c * k_scale, s * k_scale])])
    layouts = []
    for d in ATT_DILATIONS:
        p = pos.reshape(SEQ // d, d).T.reshape(SEQ)
        c, s = cos_sin(ATT_HEAD_DIM, p)
        layouts.append(jnp.stack([jnp.concatenate([c, c], axis=1), jnp.concatenate([-s, s], axis=1)]))
    return t256, jnp.stack(layouts)


def _retention_kernel(q_ref, k_ref, v_ref, g_ref, nw_ref, dec_ref, qd_ref, kd_ref, cd_ref, o_ref, st_ref):
    C = RET_CHUNK
    st_ref[...] = jnp.zeros_like(st_ref)
    dec = dec_ref[...]
    qd = qd_ref[...]
    kd = kd_ref[...]
    cd = cd_ref[...]
    nw = nw_ref[...]
    for n in range(SEQ // C):
        rows = slice(n * C, (n + 1) * C)
        q = q_ref[rows, :]
        k = k_ref[rows, :]
        v = v_ref[rows, :]
        s = lax.dot_general(q, k, (((1,), (1,)), ((), ())), preferred_element_type=F32) * dec
        intra = jnp.dot(s.astype(BF16), v, preferred_element_type=F32)
        state = st_ref[...]
        inter = jnp.dot(q, state.astype(BF16), preferred_element_type=F32) * qd
        kdec = (k.astype(F32) * kd).astype(BF16)
        kv = lax.dot_general(kdec, v, (((0,), (0,)), ((), ())), preferred_element_type=F32)
        st_ref[...] = state * cd + kv
        r = intra + inter
        mu = jnp.mean(r, axis=-1, keepdims=True)
        rc = r - mu
        var = jnp.mean(rc * rc, axis=-1, keepdims=True)
        rn = rc * lax.rsqrt(var + HEAD_NORM_EPS)
        o_ref[rows, :] = (rn * nw * g_ref[rows, :].astype(F32)).astype(BF16)


def _retention_tables():
    C = RET_CHUNK
    lg = jnp.log(1.0 - 2.0 ** (-5.0 - jnp.arange(RET_HEADS, dtype=F32)))
    idx = jnp.arange(C, dtype=F32)
    diff = idx[:, None] - idx[None, :]
    intra = jnp.where(diff[None] >= 0, jnp.exp(jnp.maximum(diff, 0.0)[None] * lg[:, None, None]), 0.0)
    q_decay = jnp.exp((idx + 1.0)[None, :] * lg[:, None])
    k_decay = jnp.exp((C - 1.0 - idx)[None, :] * lg[:, None])
    chunk_decay = jnp.exp(C * lg)
    bcast = (RET_HEADS, C, RET_HEAD_DIM)
    return (intra.astype(F32),
            jnp.broadcast_to(q_decay[:, :, None], bcast).astype(F32),
            jnp.broadcast_to(k_decay[:, :, None], bcast).astype(F32),
            jnp.broadcast_to(chunk_decay[:, None, None], (RET_HEADS, 1, RET_HEAD_DIM)).astype(F32))


def _retention(z, ret_norm_w, tables):
    T = z.shape[0]
    hd = RET_HEAD_DIM
    per_head = RET_HEADS
    dec, qd, kd, cd = tables

    def zspec(col_tile):
        return pl.BlockSpec((SEQ, hd), lambda b, h: (b, col_tile * per_head + h))

    def tspec(shape):
        return pl.BlockSpec((None,) + shape, lambda b, h: (h, 0, 0))

    return pl.pallas_call(
        _retention_kernel,
        grid=(T // SEQ, RET_HEADS),
        in_specs=[zspec(_COL_RQ), zspec(_COL_RK), zspec(_COL_RV), zspec(_COL_RG),
                  pl.BlockSpec((1, hd), lambda b, h: (0, h)),
                  tspec((RET_CHUNK, RET_CHUNK)), tspec((RET_CHUNK, hd)), tspec((RET_CHUNK, hd)),
                  tspec((1, hd))],
        out_specs=pl.BlockSpec((SEQ, hd), lambda b, h: (b, h)),
        out_shape=jax.ShapeDtypeStruct((T, RET_HEADS * hd), BF16),
        scratch_shapes=[pltpu.VMEM((hd, hd), F32)],
        compiler_params=_cparams(("arbitrary", "arbitrary")),
        name="retention",
    )(z, z, z, z, ret_norm_w.reshape(1, RET_HEADS * hd), dec, qd, kd, cd)


def _attention_kernel(q0, k0, v0, q1, k1, v1, q2, k2, v2, g_ref, o_ref, on_ref, ln_ref):
    blk = ATT_BLOCK
    scale = ATT_HEAD_DIM ** -0.5
    row = lax.broadcasted_iota(jnp.int32, (blk, 2 * blk), 0)
    col = lax.broadcasted_iota(jnp.int32, (blk, 2 * blk), 1)
    in_prev = col < blk
    valid_pair = jnp.where(in_prev, -1, 1) * (row - jnp.where(in_prev, col, col - blk)) >= 0
    valid_first = (lax.broadcasted_iota(jnp.int32, (blk, blk), 1)
                   <= lax.broadcasted_iota(jnp.int32, (blk, blk), 0))
    neg_inf = jnp.float32(-jnp.inf)

    for g, (d, q_ref, k_ref, v_ref) in enumerate(
            zip(ATT_DILATIONS, (q0, q1, q2), (k0, k1, k2), (v0, v1, v2))):
        L = SEQ // d
        for r in range(d):
            for i in range(L // blk):
                q_rows = slice(r * L + i * blk, r * L + (i + 1) * blk)
                kv_rows = slice(r * L + max(i - 1, 0) * blk, r * L + (i + 1) * blk)
                valid = valid_first if i == 0 else valid_pair
                kk = k_ref[kv_rows, :]
                vv = v_ref[kv_rows, :]
                s = lax.dot_general(q_ref[q_rows, :], kk, (((1,), (1,)), ((), ())),
                                    preferred_element_type=F32) * scale
                s = jnp.where(valid, s, neg_inf)
                m = jnp.max(s, axis=-1, keepdims=True)
                p = jnp.exp(s - m)
                den = jnp.sum(p, axis=-1, keepdims=True)
                o = jnp.dot(p.astype(BF16), vv, preferred_element_type=F32) / den
                lse = m + jnp.log(den)
                dst = pl.ds(i * blk * d + r, blk, stride=d) if d > 1 else pl.ds(i * blk, blk)
                on_ref[g, dst, :] = o
                ln_ref[g, dst, :] = jnp.broadcast_to(lse, (blk, ATT_HEAD_DIM))

    l0, l1, l2 = ln_ref[0], ln_ref[1], ln_ref[2]
    m = jnp.maximum(jnp.maximum(l0, l1), l2)
    e0, e1, e2 = jnp.exp(l0 - m), jnp.exp(l1 - m), jnp.exp(l2 - m)
    a = (e0 * on_ref[0] + e1 * on_ref[1] + e2 * on_ref[2]) / (e0 + e1 + e2)
    o_ref[...] = (a * g_ref[...].astype(F32)).astype(BF16)


def _attention(z):
    T = z.shape[0]
    hd = ATT_HEAD_DIM
    per_tile = _PROJ_TN // hd

    def zspec(col_tile):
        return pl.BlockSpec((SEQ, hd), lambda b, h: (b, col_tile * per_tile + h))

    in_specs = []
    for g in range(len(ATT_DILATIONS)):
        in_specs += [zspec(_COL_AQ + g), zspec(_COL_AK + g), zspec(_COL_AV + g)]
    in_specs.append(zspec(_COL_AG))
    n_g = len(ATT_DILATIONS)
    return pl.pallas_call(
        _attention_kernel,
        grid=(T // SEQ, ATT_HEADS_PER_GROUP),
        in_specs=in_specs,
        out_specs=pl.BlockSpec((SEQ, hd), lambda b, h: (b, h)),
        out_shape=jax.ShapeDtypeStruct((T, ATT_HEADS_PER_GROUP * hd), BF16),
        scratch_shapes=[pltpu.VMEM((n_g, SEQ, hd), F32), pltpu.VMEM((n_g, SEQ, hd), F32)],
        compiler_params=_cparams(("arbitrary", "arbitrary")),
        name="attention",
    )(*([z] * 10))


_CONV_ROWS = 256
_CONV_HALO = 32


def _conv_kernel(cl_ref, cs_ref, g_ref, dw_ref, dwb_ref, lnw_ref, lnb_ref, o_ref, pad_ref):
    R, H = _CONV_ROWS, _CONV_HALO

    @pl.when(pl.program_id(1) == 0)
    def _():
        pad_ref[0:H, :] = jnp.zeros((H, D_MODEL), F32)

    pad_ref[H:H + R, :] = cl_ref[...].astype(F32) * cs_ref[...].astype(F32)
    lead = H - (CONV_TAPS - 1)
    acc = pad_ref[lead:lead + R, :] * dw_ref[0:1, :]
    for k in range(1, CONV_TAPS):
        acc = acc + pad_ref[lead + k:lead + k + R, :] * dw_ref[k:k + 1, :]
    tail = pad_ref[R:R + H, :]
    pad_ref[0:H, :] = tail
    c = acc + dwb_ref[...]
    mu = jnp.mean(c, axis=-1, keepdims=True)
    cc = c - mu
    var = jnp.mean(cc * cc, axis=-1, keepdims=True)
    y = cc * lax.rsqrt(var + HEAD_NORM_EPS) * lnw_ref[...] + lnb_ref[...]
    y = y * jax.nn.sigmoid(y)
    o_ref[...] = (y * g_ref[...].astype(F32)).astype(BF16)


def _conv(z, dw_w, dw_b, ln_w, ln_b):
    T = z.shape[0]
    R = _CONV_ROWS
    n_r = SEQ // R

    def zspec(col_tile):
        return pl.BlockSpec((R, D_MODEL), lambda b, j: (b * n_r + j, col_tile))

    def vec():
        return pl.BlockSpec((1, D_MODEL), lambda b, j: (0, 0))

    return pl.pallas_call(
        _conv_kernel,
        grid=(T // SEQ, n_r),
        in_specs=[zspec(_COL_CLIN), zspec(_COL_CGATE), zspec(_COL_CG),
                  pl.BlockSpec((CONV_TAPS, D_MODEL), lambda b, j: (0, 0)), vec(), vec(), vec()],
        out_specs=pl.BlockSpec((R, D_MODEL), lambda b, j: (b * n_r + j, 0)),
        out_shape=jax.ShapeDtypeStruct((T, D_MODEL), BF16),
        scratch_shapes=[pltpu.VMEM((R + _CONV_HALO, D_MODEL), F32)],
        compiler_params=_cparams(("arbitrary", "arbitrary")),
        name="conv",
    )(z, z, z, dw_w, dw_b.reshape(1, D_MODEL), ln_w.reshape(1, D_MODEL), ln_b.reshape(1, D_MODEL))


_OUT_TM = 512


def _out_proj_kernel(final, r_ref, a_ref, c_ref, g0_ref, g1_ref, g2_ref, x_ref,
                     wr_ref, wa_ref, wc_ref, wo_ref, fw_ref, o_ref):
    def branch(in_ref, w_ref, gate_ref):
        y = jnp.dot(in_ref[...], w_ref[...], preferred_element_type=F32)
        return gate_ref[...].astype(F32) * y

    merged = branch(r_ref, wr_ref, g0_ref) + branch(a_ref, wa_ref, g1_ref) + branch(c_ref, wc_ref, g2_ref)
    out = x_ref[...] + jnp.dot(merged.astype(BF16), wo_ref[...], preferred_element_type=F32)
    if final:
        out = out * lax.rsqrt(jnp.mean(out * out, axis=-1, keepdims=True) + NORM_EPS) * fw_ref[...]
    o_ref[...] = out


def _out_proj(rg, ag, cg, z, xf, w_r, w_a, w_c, w_o, final_w, final):
    T = xf.shape[0]
    tm = _OUT_TM

    def rows(col_tile=0):
        return pl.BlockSpec((tm, D_MODEL), lambda i: (i, col_tile))

    def weight():
        return pl.BlockSpec((D_MODEL, D_MODEL), lambda i: (0, 0))

    return pl.pallas_call(
        functools.partial(_out_proj_kernel, final),
        grid=(T // tm,),
        in_specs=[rows(), rows(), rows(), rows(_COL_MG), rows(_COL_MG + 1), rows(_COL_MG + 2), rows(),
                  weight(), weight(), weight(), weight(),
                  pl.BlockSpec((1, D_MODEL), lambda i: (0, 0))],
        out_specs=rows(),
        out_shape=jax.ShapeDtypeStruct((T, D_MODEL), F32),
        compiler_params=_cparams(("arbitrary",)),
        name="out_proj_final" if final else "out_proj",
    )(rg, ag, cg, z, z, z, xf, w_r, w_a, w_c, w_o, final_w.reshape(1, D_MODEL))


def kernel(x, norm_w, w_in, b_in, ret_norm_w, ret_w_o, att_w_o, conv_dw_w, conv_dw_b,
           conv_norm_w, conv_norm_b, conv_w_o, w_out, final_norm_w):
    B, S, D = x.shape
    assert (S, D) == (SEQ, D_MODEL) and w_in.shape[-1] == IN_WIDTH
    depth = w_in.shape[0]
    t256, t128 = _rope_tables()
    ret_tables = _retention_tables()
    xf = x.reshape(B * S, D)
    for l in range(depth):
        hs = _norm_perm(xf, norm_w[l])
        z = _in_proj(hs, w_in[l].astype(BF16), b_in[l], t256, t128)
        rg = _retention(z, ret_norm_w[l], ret_tables)
        ag = _attention(z)
        cg = _conv(z, conv_dw_w[l], conv_dw_b[l], conv_norm_w[l], conv_norm_b[l])
        xf = _out_proj(rg, ag, cg, z, xf,
                       ret_w_o[l].astype(BF16), att_w_o[l].astype(BF16), conv_w_o[l].astype(BF16),
                       w_out[l].astype(BF16), final_norm_w, final=(l == depth - 1))
    return xf.reshape(B, S, D)
```

```python
import functools

import numpy as np
import jax
import jax.numpy as jnp
from jax import lax
from jax.experimental import pallas as pl
from jax.experimental.pallas import tpu as pltpu

F32 = jnp.float32
BF16 = jnp.bfloat16

D_MODEL = 1024
SEQ = 2048
ROPE_THETA = 10000.0
NORM_EPS = 1e-6
HEAD_NORM_EPS = 1e-5
RET_HEADS = 4
RET_HEAD_DIM = 256
RET_CHUNK = 128
ATT_DILATIONS = (1, 4, 16)
ATT_HEADS_PER_GROUP = 8
ATT_HEAD_DIM = 128
ATT_BLOCK = 128
CONV_TAPS = 31
IN_WIDTH = 20480

LANES = 128
SUBLANES = 8
VMEM_LIMIT = 56 * 1024 * 1024

_COL_RQ, _COL_RK, _COL_RV, _COL_RG = 0, 1, 2, 3
_COL_AQ, _COL_AK, _COL_AV, _COL_AG = 4, 7, 10, 13
_COL_CLIN, _COL_CGATE, _COL_CG, _COL_MG = 14, 15, 16, 17

_EPI_ROPE256, _EPI_NONE, _EPI_SILU, _EPI_SIGMOID, _EPI_ROPE128 = 0, 1, 2, 3, 4

_TILE_PLAN = (
    (_COL_RQ, 0, _EPI_ROPE256, 0), (_COL_RK, 0, _EPI_ROPE256, 1),
    (_COL_RV, 0, _EPI_NONE, 0), (_COL_RG, 0, _EPI_SILU, 0),
    (_COL_AQ, 0, _EPI_ROPE128, 0), (_COL_AK, 0, _EPI_ROPE128, 0), (_COL_AV, 0, _EPI_NONE, 0),
    (_COL_AG, 0, _EPI_SILU, 0), (_COL_CLIN, 0, _EPI_NONE, 0), (_COL_CGATE, 0, _EPI_SIGMOID, 0),
    (_COL_CG, 0, _EPI_SILU, 0),
    (_COL_MG, 0, _EPI_SIGMOID, 0), (_COL_MG + 1, 0, _EPI_SIGMOID, 0), (_COL_MG + 2, 0, _EPI_SIGMOID, 0),
    (_COL_AQ + 1, 1, _EPI_ROPE128, 0), (_COL_AK + 1, 1, _EPI_ROPE128, 0), (_COL_AV + 1, 1, _EPI_NONE, 0),
    (_COL_AQ + 2, 2, _EPI_ROPE128, 0), (_COL_AK + 2, 2, _EPI_ROPE128, 0), (_COL_AV + 2, 2, _EPI_NONE, 0),
)


def _cparams(sem):
    return pltpu.CompilerParams(dimension_semantics=sem, vmem_limit_bytes=VMEM_LIMIT)


def _norm_perm_kernel(x_ref, w_ref, hs_ref, slab_ref):
    x = x_ref[...]
    h = x * lax.rsqrt(jnp.mean(x * x, axis=-1, keepdims=True) + NORM_EPS) * w_ref[...]
    hs_ref[0] = h.astype(BF16)
    n_slab = D_MODEL // LANES
    for c in range(n_slab):
        slab_ref[c] = h[:, c * LANES:(c + 1) * LANES]
    for li, d in enumerate(ATT_DILATIONS[1:], start=1):
        L = SEQ // d
        for r in range(d):
            for c in range(n_slab):
                hs_ref[li, r * L:(r + 1) * L, c * LANES:(c + 1) * LANES] = (
                    slab_ref[c, pl.ds(r, L, stride=d), :].astype(BF16))


def _norm_perm(xf, norm_w):
    T = xf.shape[0]
    return pl.pallas_call(
        _norm_perm_kernel,
        grid=(T // SEQ,),
        in_specs=[pl.BlockSpec((SEQ, D_MODEL), lambda b: (b, 0)),
                  pl.BlockSpec((1, D_MODEL), lambda b: (0, 0))],
        out_specs=pl.BlockSpec((3, SEQ, D_MODEL), lambda b: (0, b, 0)),
        out_shape=jax.ShapeDtypeStruct((3, T, D_MODEL), BF16),
        scratch_shapes=[pltpu.VMEM((D_MODEL // LANES, SEQ, LANES), F32)],
        compiler_params=_cparams(("arbitrary",)),
        name="norm_perm",
    )(xf, norm_w.reshape(1, D_MODEL))


_PROJ_TM = 2048
_PROJ_TN = 1024
_PROJ_NC = 256


def _in_proj_kernel(tbl_ref, hs_ref, w_ref, b_ref, t256_ref, t128_ref, o_ref):
    kind = tbl_ref[2, pl.program_id(1)]

    def chunk(nc):
        cols = slice(nc * _PROJ_NC, (nc + 1) * _PROJ_NC)
        z = jnp.dot(hs_ref[...], w_ref[:, cols], preferred_element_type=F32)
        return z + b_ref[:, cols]

    def emit(epilogue):
        for nc in range(_PROJ_TN // _PROJ_NC):
            cols = slice(nc * _PROJ_NC, (nc + 1) * _PROJ_NC)
            o_ref[:, cols] = epilogue(chunk(nc)).astype(BF16)

    def rope256(z):
        cos, sin = t256_ref[0], t256_ref[1]
        x1, x2 = z[:, :LANES], z[:, LANES:]
        return jnp.concatenate([x1 * cos - x2 * sin, x2 * cos + x1 * sin], axis=1)

    def rope128(z):
        cos_full, sin_signed = t128_ref[0], t128_ref[1]
        halves = []
        for hh in range(_PROJ_NC // LANES):
            x = z[:, hh * LANES:(hh + 1) * LANES]
            halves.append(x * cos_full + pltpu.roll(x, LANES // 2, 1) * sin_signed)
        return jnp.concatenate(halves, axis=1)

    epilogues = {
        _EPI_ROPE256: rope256,
        _EPI_NONE: lambda z: z,
        _EPI_SILU: lambda z: z * jax.nn.sigmoid(z),
        _EPI_SIGMOID: jax.nn.sigmoid,
        _EPI_ROPE128: rope128,
    }
    for k, fn in epilogues.items():
        pl.when(kind == k)(functools.partial(emit, fn))


def _in_proj(hs, w_bf16, b_in, t256, t128):
    T = hs.shape[1]
    tbl = jnp.asarray(np.array(_TILE_PLAN, dtype=np.int32).T)
    n_pos = SEQ // _PROJ_TM
    grid_spec = pltpu.PrefetchScalarGridSpec(
        num_scalar_prefetch=1,
        grid=(T // _PROJ_TM, IN_WIDTH // _PROJ_TN),
        in_specs=[
            pl.BlockSpec((None, _PROJ_TM, D_MODEL), lambda i, j, t: (t[1, j], i, 0)),
            pl.BlockSpec((D_MODEL, _PROJ_TN), lambda i, j, t: (0, t[0, j])),
            pl.BlockSpec((1, _PROJ_TN), lambda i, j, t: (0, t[0, j])),
            pl.BlockSpec((None, 2, _PROJ_TM, LANES), lambda i, j, t: (t[3, j], 0, i % n_pos, 0)),
            pl.BlockSpec((None, 2, _PROJ_TM, LANES), lambda i, j, t: (t[1, j], 0, i % n_pos, 0)),
        ],
        out_specs=pl.BlockSpec((_PROJ_TM, _PROJ_TN), lambda i, j, t: (i, t[0, j])),
    )
    return pl.pallas_call(
        _in_proj_kernel,
        grid_spec=grid_spec,
        out_shape=jax.ShapeDtypeStruct((T, IN_WIDTH), BF16),
        compiler_params=_cparams(("arbitrary", "arbitrary")),
        name="in_proj",
    )(tbl, hs, w_bf16, b_in.reshape(1, IN_WIDTH), t256, t128)


def _rope_tables():
    pos = jnp.arange(SEQ, dtype=jnp.int32)

    def cos_sin(hd, p):
        inv = ROPE_THETA ** (-jnp.arange(0, hd, 2, dtype=F32) / hd)
        ang = p.astype(F32)[:, None] * inv[None, :]
        return jnp.cos(ang), jnp.sin(ang)

    c, s = cos_sin(RET_HEAD_DIM, pos)
    k_scale = RET_HEAD_DIM ** -0.5
    t256 = jnp.stack([jnp.stack([c, s]), jnp.stack([c * k_scale, s * k_scale])])
    layouts = []
    for d in ATT_DILATIONS:
        p = pos.reshape(SEQ // d, d).T.reshape(SEQ)
        c, s = cos_sin(ATT_HEAD_DIM, p)
        layouts.append(jnp.stack([jnp.concatenate([c, c], axis=1), jnp.concatenate([-s, s], axis=1)]))
    return t256, jnp.stack(layouts)


def _retention_kernel(q_ref, k_ref, v_ref, g_ref, nw_ref, dec_ref, qd_ref, kd_ref, cd_ref, o_ref, st_ref):
    C = RET_CHUNK
    st_ref[...] = jnp.zeros_like(st_ref)
    dec = dec_ref[...]
    qd = qd_ref[...]
    kd = kd_ref[...]
    cd = cd_ref[...]
    nw = nw_ref[...]
    for n in range(SEQ // C):
        rows = slice(n * C, (n + 1) * C)
        q = q_ref[rows, :]
        k = k_ref[rows, :]
        v = v_ref[rows, :]
        s = lax.dot_general(q, k, (((1,), (1,)), ((), ())), preferred_element_type=F32) * dec
        intra = jnp.dot(s.astype(BF16), v, preferred_element_type=F32)
        state = st_ref[...]
        inter = jnp.dot(q, state.astype(BF16), preferred_element_type=F32) * qd
        kdec = (k.astype(F32) * kd).astype(BF16)
        kv = lax.dot_general(kdec, v, (((0,), (0,)), ((), ())), preferred_element_type=F32)
        st_ref[...] = state * cd + kv
        r = intra + inter
        mu = jnp.mean(r, axis=-1, keepdims=True)
        rc = r - mu
        var = jnp.mean(rc * rc, axis=-1, keepdims=True)
        rn = rc * lax.rsqrt(var + HEAD_NORM_EPS)
        o_ref[rows, :] = (rn * nw * g_ref[rows, :].astype(F32)).astype(BF16)


def _retention_tables():
    C = RET_CHUNK
    lg = jnp.log(1.0 - 2.0 ** (-5.0 - jnp.arange(RET_HEADS, dtype=F32)))
    idx = jnp.arange(C, dtype=F32)
    diff = idx[:, None] - idx[None, :]
    intra = jnp.where(diff[None] >= 0, jnp.exp(jnp.maximum(diff, 0.0)[None] * lg[:, None, None]), 0.0)
    q_decay = jnp.exp((idx + 1.0)[None, :] * lg[:, None])
    k_decay = jnp.exp((C - 1.0 - idx)[None, :] * lg[:, None])
    chunk_decay = jnp.exp(C * lg)
    bcast = (RET_HEADS, C, RET_HEAD_DIM)
    return (intra.astype(F32),
            jnp.broadcast_to(q_decay[:, :, None], bcast).astype(F32),
            jnp.broadcast_to(k_decay[:, :, None], bcast).astype(F32),
            jnp.broadcast_to(chunk_decay[:, None, None], (RET_HEADS, 1, RET_HEAD_DIM)).astype(F32))


def _retention(z, ret_norm_w, tables):
    T = z.shape[0]
    hd = RET_HEAD_DIM
    per_head = RET_HEADS
    dec, qd, kd, cd = tables

    def zspec(col_tile):
        return pl.BlockSpec((SEQ, hd), lambda b, h: (b, col_tile * per_head + h))

    def tspec(shape):
        return pl.BlockSpec((None,) + shape, lambda b, h: (h, 0, 0))

    return pl.pallas_call(
        _retention_kernel,
        grid=(T // SEQ, RET_HEADS),
        in_specs=[zspec(_COL_RQ), zspec(_COL_RK), zspec(_COL_RV), zspec(_COL_RG),
                  pl.BlockSpec((1, hd), lambda b, h: (0, h)),
                  tspec((RET_CHUNK, RET_CHUNK)), tspec((RET_CHUNK, hd)), tspec((RET_CHUNK, hd)),
                  tspec((1, hd))],
        out_specs=pl.BlockSpec((SEQ, hd), lambda b, h: (b, h)),
        out_shape=jax.ShapeDtypeStruct((T, RET_HEADS * hd), BF16),
        scratch_shapes=[pltpu.VMEM((hd, hd), F32)],
        compiler_params=_cparams(("arbitrary", "arbitrary")),
        name="retention",
    )(z, z, z, z, ret_norm_w.reshape(1, RET_HEADS * hd), dec, qd, kd, cd)


def _attention_kernel(q0, k0, v0, q1, k1, v1, q2, k2, v2, g_ref, o_ref,
                      on_ref, ln_ref, s_ref, p_ref, m_ref):
    blk = ATT_BLOCK
    hd = ATT_HEAD_DIM
    n_blk = SEQ // blk
    scale = hd ** -0.5
    neg_inf = jnp.float32(-jnp.inf)
    row = lax.broadcasted_iota(jnp.int32, (blk, 2 * blk), 0)
    col = lax.broadcasted_iota(jnp.int32, (blk, 2 * blk), 1)
    in_prev = col < blk
    valid = jnp.where(in_prev, -1, 1) * (row - jnp.where(in_prev, col, col - blk)) >= 0
    bias = jnp.where(valid, jnp.float32(0.0), neg_inf)
    contract_last = (((1,), (1,)), ((), ()))

    for g, (d, q_ref, k_ref, v_ref) in enumerate(
            zip(ATT_DILATIONS, (q0, q1, q2), (k0, k1, k2), (v0, v1, v2))):
        per_seq = n_blk // d
        for bi in range(n_blk):
            q = q_ref[bi * blk:(bi + 1) * blk, :]
            if bi % per_seq == 0:
                s_ref[bi, :, :blk] = jnp.full((blk, blk), neg_inf, F32)
                s_ref[bi, :, blk:] = lax.dot_general(q, k_ref[bi * blk:(bi + 1) * blk, :], contract_last,
                                                     preferred_element_type=F32)
            else:
                s_ref[bi] = lax.dot_general(q, k_ref[(bi - 1) * blk:(bi + 1) * blk, :], contract_last,
                                            preferred_element_type=F32)
        s = s_ref[...] * scale + bias[None]
        m = jnp.max(s, axis=-1, keepdims=True)
        p_ref[...] = jnp.exp(s - m).astype(BF16)
        m_ref[...] = jnp.broadcast_to(m, (n_blk, blk, hd))
        for bi in range(n_blk):
            first = bi % per_seq == 0
            vv = v_ref[(bi if first else bi - 1) * blk:(bi + 1) * blk, :]
            v_aug = jnp.concatenate([vv, jnp.ones_like(vv)], axis=1)
            pv = jnp.dot(p_ref[bi, :, blk:] if first else p_ref[bi], v_aug, preferred_element_type=F32)
            den = pv[:, hd:]
            r, i = divmod(bi, per_seq)
            dst = pl.ds(i * blk * d + r, blk, stride=d) if d > 1 else pl.ds(i * blk, blk)
            on_ref[g, dst, :] = pv[:, :hd] / den
            ln_ref[g, dst, :] = m_ref[bi] + jnp.log(den)

    l0, l1, l2 = ln_ref[0], ln_ref[1], ln_ref[2]
    m = jnp.maximum(jnp.maximum(l0, l1), l2)
    e0, e1, e2 = jnp.exp(l0 - m), jnp.exp(l1 - m), jnp.exp(l2 - m)
    a = (e0 * on_ref[0] + e1 * on_ref[1] + e2 * on_ref[2]) / (e0 + e1 + e2)
    o_ref[...] = (a * g_ref[...].astype(F32)).astype(BF16)


def _attention(z):
    T = z.shape[0]
    hd = ATT_HEAD_DIM
    per_tile = _PROJ_TN // hd

    def zspec(col_tile):
        return pl.BlockSpec((SEQ, hd), lambda b, h: (b, col_tile * per_tile + h))

    in_specs = []
    for g in range(len(ATT_DILATIONS)):
        in_specs += [zspec(_COL_AQ + g), zspec(_COL_AK + g), zspec(_COL_AV + g)]
    in_specs.append(zspec(_COL_AG))
    n_g = len(ATT_DILATIONS)
    return pl.pallas_call(
        _attention_kernel,
        grid=(T // SEQ, ATT_HEADS_PER_GROUP),
        in_specs=in_specs,
        out_specs=pl.BlockSpec((SEQ, hd), lambda b, h: (b, h)),
        out_shape=jax.ShapeDtypeStruct((T, ATT_HEADS_PER_GROUP * hd), BF16),
        scratch_shapes=[pltpu.VMEM((n_g, SEQ, hd), F32), pltpu.VMEM((n_g, SEQ, hd), F32),
                        pltpu.VMEM((SEQ // ATT_BLOCK, ATT_BLOCK, 2 * ATT_BLOCK), F32),
                        pltpu.VMEM((SEQ // ATT_BLOCK, ATT_BLOCK, 2 * ATT_BLOCK), BF16),
                        pltpu.VMEM((SEQ // ATT_BLOCK, ATT_BLOCK, hd), F32)],
        compiler_params=_cparams(("arbitrary", "arbitrary")),
        name="attention",
    )(*([z] * 10))


_CONV_ROWS = 512
_CONV_HALO = 32
_CONV_NORM_ROWS = 128


def _conv_kernel(cl_ref, cs_ref, g_ref, dw_ref, dwb_ref, lnw_ref, lnb_ref, o_ref, pad_ref, acc_ref):
    R, H = _CONV_ROWS, _CONV_HALO
    n_slab = D_MODEL // LANES

    @pl.when(pl.program_id(1) == 0)
    def _():
        pad_ref[:, 0:H, :] = jnp.zeros((n_slab, H, LANES), F32)

    lead = H - (CONV_TAPS - 1)
    for c in range(n_slab):
        cols = slice(c * LANES, (c + 1) * LANES)
        pad_ref[c, H:H + R, :] = cl_ref[:, cols].astype(F32) * cs_ref[:, cols].astype(F32)
    for c in range(n_slab):
        cols = slice(c * LANES, (c + 1) * LANES)
        acc = pad_ref[c, lead:lead + R, :] * dw_ref[0:1, cols]
        for k in range(1, CONV_TAPS):
            acc = acc + pad_ref[c, lead + k:lead + k + R, :] * dw_ref[k:k + 1, cols]
        acc_ref[:, cols] = acc
    for c in range(n_slab):
        pad_ref[c, 0:H, :] = pad_ref[c, R:R + H, :]

    def norm_rows(i, carry):
        rows = pl.ds(pl.multiple_of(i * _CONV_NORM_ROWS, _CONV_NORM_ROWS), _CONV_NORM_ROWS)
        c = acc_ref[rows, :] + dwb_ref[...]
        mu = jnp.mean(c, axis=-1, keepdims=True)
        cc = c - mu
        var = jnp.mean(cc * cc, axis=-1, keepdims=True)
        y = cc * lax.rsqrt(var + HEAD_NORM_EPS) * lnw_ref[...] + lnb_ref[...]
        y = y * jax.nn.sigmoid(y)
        o_ref[rows, :] = (y * g_ref[rows, :].astype(F32)).astype(BF16)
        return carry

    lax.fori_loop(0, R // _CONV_NORM_ROWS, norm_rows, 0)


def _conv(z, dw_w, dw_b, ln_w, ln_b):
    T = z.shape[0]
    R = _CONV_ROWS
    n_r = SEQ // R

    def zspec(col_tile):
        return pl.BlockSpec((R, D_MODEL), lambda b, j: (b * n_r + j, col_tile))

    def vec():
        return pl.BlockSpec((1, D_MODEL), lambda b, j: (0, 0))

    return pl.pallas_call(
        _conv_kernel,
        grid=(T // SEQ, n_r),
        in_specs=[zspec(_COL_CLIN), zspec(_COL_CGATE), zspec(_COL_CG),
                  pl.BlockSpec((CONV_TAPS, D_MODEL), lambda b, j: (0, 0)), vec(), vec(), vec()],
        out_specs=pl.BlockSpec((R, D_MODEL), lambda b, j: (b * n_r + j, 0)),
        out_shape=jax.ShapeDtypeStruct((T, D_MODEL), BF16),
        scratch_shapes=[pltpu.VMEM((D_MODEL // LANES, R + _CONV_HALO, LANES), F32),
                        pltpu.VMEM((R, D_MODEL), F32)],
        compiler_params=_cparams(("arbitrary", "arbitrary")),
        name="conv",
    )(z, z, z, dw_w, dw_b.reshape(1, D_MODEL), ln_w.reshape(1, D_MODEL), ln_b.reshape(1, D_MODEL))


_OUT_TM = 512


def _out_proj_kernel(final, r_ref, a_ref, c_ref, g0_ref, g1_ref, g2_ref, x_ref,
                     wr_ref, wa_ref, wc_ref, wo_ref, fw_ref, o_ref):
    def branch(in_ref, w_ref, gate_ref):
        y = jnp.dot(in_ref[...], w_ref[...], preferred_element_type=F32)
        return gate_ref[...].astype(F32) * y

    merged = branch(r_ref, wr_ref, g0_ref) + branch(a_ref, wa_ref, g1_ref) + branch(c_ref, wc_ref, g2_ref)
    out = x_ref[...] + jnp.dot(merged.astype(BF16), wo_ref[...], preferred_element_type=F32)
    if final:
        out = out * lax.rsqrt(jnp.mean(out * out, axis=-1, keepdims=True) + NORM_EPS) * fw_ref[...]
    o_ref[...] = out


def _out_proj(rg, ag, cg, z, xf, w_r, w_a, w_c, w_o, final_w, final):
    T = xf.shape[0]
    tm = _OUT_TM

    def rows(col_tile=0):
        return pl.BlockSpec((tm, D_MODEL), lambda i: (i, col_tile))

    def weight():
        return pl.BlockSpec((D_MODEL, D_MODEL), lambda i: (0, 0))

    return pl.pallas_call(
        functools.partial(_out_proj_kernel, final),
        grid=(T // tm,),
        in_specs=[rows(), rows(), rows(), rows(_COL_MG), rows(_COL_MG + 1), rows(_COL_MG + 2), rows(),
                  weight(), weight(), weight(), weight(),
                  pl.BlockSpec((1, D_MODEL), lambda i: (0, 0))],
        out_specs=rows(),
        out_shape=jax.ShapeDtypeStruct((T, D_MODEL), F32),
        compiler_params=_cparams(("arbitrary",)),
        name="out_proj_final" if final else "out_proj",
    )(rg, ag, cg, z, z, z, xf, w_r, w_a, w_c, w_o, final_w.reshape(1, D_MODEL))


def kernel(x, norm_w, w_in, b_in, ret_norm_w, ret_w_o, att_w_o, conv_dw_w, conv_dw_b,
           conv_norm_w, conv_norm_b, conv_w_o, w_out, final_norm_w):
    B, S, D = x.shape
    assert (S, D) == (SEQ, D_MODEL) and w_in.shape[-1] == IN_WIDTH
    depth = w_in.shape[0]
    t256, t128 = _rope_tables()
    ret_tables = _retention_tables()
    xf = x.reshape(B * S, D)
    for l in range(depth):
        hs = _norm_perm(xf, norm_w[l])
        z = _in_proj(hs, w_in[l].astype(BF16), b_in[l], t256, t128)
        rg = _retention(z, ret_norm_w[l], ret_tables)
        ag = _attention(z)
        cg = _conv(z, conv_dw_w[l], conv_dw_b[l], conv_norm_w[l], conv_norm_b[l])
        xf = _out_proj(rg, ag, cg, z, xf,
                       ret_w_o[l].astype(BF16), att_w_o[l].astype(BF16), conv_w_o[l].astype(BF16),
                       w_out[l].astype(BF16), final_norm_w, final=(l == depth - 1))
    return xf.reshape(B, S, D)
```

```python
import functools

import numpy as np
import jax
import jax.numpy as jnp
from jax import lax
from jax.experimental import pallas as pl
from jax.experimental.pallas import tpu as pltpu

F32 = jnp.float32
BF16 = jnp.bfloat16

D_MODEL = 1024
SEQ = 2048
ROPE_THETA = 10000.0
NORM_EPS = 1e-6
HEAD_NORM_EPS = 1e-5
RET_HEADS = 4
RET_HEAD_DIM = 256
RET_CHUNK = 128
ATT_DILATIONS = (1, 4, 16)
ATT_HEADS_PER_GROUP = 8
ATT_HEAD_DIM = 128
ATT_BLOCK = 128
CONV_TAPS = 31
IN_WIDTH = 20480

LANES = 128
SUBLANES = 8
VMEM_LIMIT = 56 * 1024 * 1024

_COL_RQ, _COL_RK, _COL_RV, _COL_RG = 0, 1, 2, 3
_COL_AQ, _COL_AK, _COL_AV, _COL_AG = 4, 7, 10, 13
_COL_CLIN, _COL_CGATE, _COL_CG, _COL_MG = 14, 15, 16, 17

_EPI_ROPE256, _EPI_NONE, _EPI_SILU, _EPI_SIGMOID, _EPI_ROPE128 = 0, 1, 2, 3, 4

_TILE_PLAN = (
    (_COL_RQ, 0, _EPI_ROPE256, 0), (_COL_RK, 0, _EPI_ROPE256, 1),
    (_COL_RV, 0, _EPI_NONE, 0), (_COL_RG, 0, _EPI_NONE, 0),
    (_COL_AQ, 0, _EPI_ROPE128, 0), (_COL_AK, 0, _EPI_ROPE128, 0), (_COL_AV, 0, _EPI_NONE, 0),
    (_COL_AG, 0, _EPI_NONE, 0), (_COL_CLIN, 0, _EPI_NONE, 0), (_COL_CGATE, 0, _EPI_SIGMOID, 0),
    (_COL_CG, 0, _EPI_SILU, 0),
    (_COL_MG, 0, _EPI_NONE, 0), (_COL_MG + 1, 0, _EPI_NONE, 0), (_COL_MG + 2, 0, _EPI_NONE, 0),
    (_COL_AQ + 1, 1, _EPI_ROPE128, 0), (_COL_AK + 1, 1, _EPI_ROPE128, 0), (_COL_AV + 1, 1, _EPI_NONE, 0),
    (_COL_AQ + 2, 2, _EPI_ROPE128, 0), (_COL_AK + 2, 2, _EPI_ROPE128, 0), (_COL_AV + 2, 2, _EPI_NONE, 0),
)


def _cparams(sem):
    return pltpu.CompilerParams(dimension_semantics=sem, vmem_limit_bytes=VMEM_LIMIT)


def _norm_perm_kernel(x_ref, w_ref, hs_ref, slab_ref):
    x = x_ref[...]
    h = x * lax.rsqrt(jnp.mean(x * x, axis=-1, keepdims=True) + NORM_EPS) * w_ref[...]
    hs_ref[0] = h.astype(BF16)
    n_slab = D_MODEL // LANES
    for c in range(n_slab):
        slab_ref[c] = h[:, c * LANES:(c + 1) * LANES]
    for li, d in enumerate(ATT_DILATIONS[1:], start=1):
        L = SEQ // d
        for r in range(d):
            for c in range(n_slab):
                hs_ref[li, r * L:(r + 1) * L, c * LANES:(c + 1) * LANES] = (
                    slab_ref[c, pl.ds(r, L, stride=d), :].astype(BF16))


def _norm_perm(xf, norm_w):
    T = xf.shape[0]
    return pl.pallas_call(
        _norm_perm_kernel,
        grid=(T // SEQ,),
        in_specs=[pl.BlockSpec((SEQ, D_MODEL), lambda b: (b, 0)),
                  pl.BlockSpec((1, D_MODEL), lambda b: (0, 0))],
        out_specs=pl.BlockSpec((3, SEQ, D_MODEL), lambda b: (0, b, 0)),
        out_shape=jax.ShapeDtypeStruct((3, T, D_MODEL), BF16),
        scratch_shapes=[pltpu.VMEM((D_MODEL // LANES, SEQ, LANES), F32)],
        compiler_params=_cparams(("arbitrary",)),
        name="norm_perm",
    )(xf, norm_w.reshape(1, D_MODEL))


_PROJ_TM = 2048
_PROJ_TN = 1024
_PROJ_NC = 256
_PROJ_MC = 1024


def _in_proj_kernel(tbl_ref, hs_ref, w_ref, b_ref, t256_ref, t128_ref, o_ref):
    kind = tbl_ref[2, pl.program_id(1)]

    def emit(epilogue):
        for nc in range(_PROJ_TN // _PROJ_NC):
            cols = slice(nc * _PROJ_NC, (nc + 1) * _PROJ_NC)
            w = w_ref[:, cols].astype(BF16)
            for mc in range(_PROJ_TM // _PROJ_MC):
                rows = slice(mc * _PROJ_MC, (mc + 1) * _PROJ_MC)
                z = jnp.dot(hs_ref[rows, :], w, preferred_element_type=F32) + b_ref[:, cols]
                o_ref[rows, cols] = epilogue(z, rows).astype(BF16)

    def rope256(z, rows):
        cos, sin = t256_ref[0, rows, :], t256_ref[1, rows, :]
        x1, x2 = z[:, :LANES], z[:, LANES:]
        return jnp.concatenate([x1 * cos - x2 * sin, x2 * cos + x1 * sin], axis=1)

    def rope128(z, rows):
        cos_full, sin_signed = t128_ref[0, rows, :], t128_ref[1, rows, :]
        halves = []
        for hh in range(_PROJ_NC // LANES):
            x = z[:, hh * LANES:(hh + 1) * LANES]
            halves.append(x * cos_full + pltpu.roll(x, LANES // 2, 1) * sin_signed)
        return jnp.concatenate(halves, axis=1)

    epilogues = {
        _EPI_ROPE256: rope256,
        _EPI_NONE: lambda z, rows: z,
        _EPI_SILU: lambda z, rows: z * jax.nn.sigmoid(z),
        _EPI_SIGMOID: lambda z, rows: jax.nn.sigmoid(z),
        _EPI_ROPE128: rope128,
    }
    for k, fn in epilogues.items():
        pl.when(kind == k)(functools.partial(emit, fn))


def _in_proj(hs, w_in, layer, b_in, t256, t128):
    T = hs.shape[1]
    tbl = jnp.asarray(np.array(_TILE_PLAN, dtype=np.int32).T)
    n_pos = SEQ // _PROJ_TM
    grid_spec = pltpu.PrefetchScalarGridSpec(
        num_scalar_prefetch=1,
        grid=(T // _PROJ_TM, IN_WIDTH // _PROJ_TN),
        in_specs=[
            pl.BlockSpec((None, _PROJ_TM, D_MODEL), lambda i, j, t: (t[1, j], i, 0)),
            pl.BlockSpec((None, D_MODEL, _PROJ_TN), lambda i, j, t: (layer, 0, t[0, j])),
            pl.BlockSpec((1, _PROJ_TN), lambda i, j, t: (0, t[0, j])),
            pl.BlockSpec((None, 2, _PROJ_TM, LANES), lambda i, j, t: (t[3, j], 0, i % n_pos, 0)),
            pl.BlockSpec((None, 2, _PROJ_TM, LANES), lambda i, j, t: (t[1, j], 0, i % n_pos, 0)),
        ],
        out_specs=pl.BlockSpec((_PROJ_TM, _PROJ_TN), lambda i, j, t: (i, t[0, j])),
    )
    return pl.pallas_call(
        _in_proj_kernel,
        grid_spec=grid_spec,
        out_shape=jax.ShapeDtypeStruct((T, IN_WIDTH), BF16),
        compiler_params=_cparams(("arbitrary", "arbitrary")),
        name="in_proj",
    )(tbl, hs, w_in, b_in.reshape(1, IN_WIDTH), t256, t128)


def _rope_tables():
    pos = jnp.arange(SEQ, dtype=jnp.int32)

    def cos_sin(hd, p):
        inv = ROPE_THETA ** (-jnp.arange(0, hd, 2, dtype=F32) / hd)
        ang = p.astype(F32)[:, None] * inv[None, :]
        return jnp.cos(ang), jnp.sin(ang)

    c, s = cos_sin(RET_HEAD_DIM, pos)
    k_scale = RET_HEAD_DIM ** -0.5
    t256 = jnp.stack([jnp.stack([c, s]), jnp.stack([c * k_scale, s * k_scale])])
    layouts = []
    for d in ATT_DILATIONS:
        p = pos.reshape(SEQ // d, d).T.reshape(SEQ)
        c, s = cos_sin(ATT_HEAD_DIM, p)
        layouts.append(jnp.stack([jnp.concatenate([c, c], axis=1), jnp.concatenate([-s, s], axis=1)]))
    return t256, jnp.stack(layouts)


def _retention_kernel(q_ref, k_ref, v_ref, g_ref, nw_ref, dec_ref, qd_ref, kd_ref, cd_ref, o_ref, st_ref):
    C = RET_CHUNK
    st_ref[...] = jnp.zeros_like(st_ref)
    dec = dec_ref[...]
    qd = qd_ref[...]
    kd = kd_ref[...]
    cd = cd_ref[...]
    nw = nw_ref[...]
    for n in range(SEQ // C):
        rows = slice(n * C, (n + 1) * C)
        q = q_ref[rows, :]
        k = k_ref[rows, :]
        v = v_ref[rows, :]
        s = lax.dot_general(q, k, (((1,), (1,)), ((), ())), preferred_element_type=F32) * dec
        intra = jnp.dot(s.astype(BF16), v, preferred_element_type=F32)
        state = st_ref[...]
        inter = jnp.dot(q, state.astype(BF16), preferred_element_type=F32) * qd
        kdec = (k.astype(F32) * kd).astype(BF16)
        kv = lax.dot_general(kdec, v, (((0,), (0,)), ((), ())), preferred_element_type=F32)
        st_ref[...] = state * cd + kv
        r = intra + inter
        mu = jnp.mean(r, axis=-1, keepdims=True)
        rc = r - mu
        var = jnp.mean(rc * rc, axis=-1, keepdims=True)
        rn = rc * lax.rsqrt(var + HEAD_NORM_EPS)
        gate = g_ref[rows, :].astype(F32)
        o_ref[rows, :] = (rn * nw * (gate * jax.nn.sigmoid(gate))).astype(BF16)


def _retention_tables():
    C = RET_CHUNK
    lg = jnp.log(1.0 - 2.0 ** (-5.0 - jnp.arange(RET_HEADS, dtype=F32)))
    idx = jnp.arange(C, dtype=F32)
    diff = idx[:, None] - idx[None, :]
    intra = jnp.where(diff[None] >= 0, jnp.exp(jnp.maximum(diff, 0.0)[None] * lg[:, None, None]), 0.0)
    q_decay = jnp.exp((idx + 1.0)[None, :] * lg[:, None])
    k_decay = jnp.exp((C - 1.0 - idx)[None, :] * lg[:, None])
    chunk_decay = jnp.exp(C * lg)
    bcast = (RET_HEADS, C, RET_HEAD_DIM)
    return (intra.astype(F32),
            jnp.broadcast_to(q_decay[:, :, None], bcast).astype(F32),
            jnp.broadcast_to(k_decay[:, :, None], bcast).astype(F32),
            jnp.broadcast_to(chunk_decay[:, None, None], (RET_HEADS, 1, RET_HEAD_DIM)).astype(F32))


def _retention(z, ret_norm_w, tables):
    T = z.shape[0]
    hd = RET_HEAD_DIM
    per_head = RET_HEADS
    dec, qd, kd, cd = tables

    def zspec(col_tile):
        return pl.BlockSpec((SEQ, hd), lambda b, h: (b, col_tile * per_head + h))

    def tspec(shape):
        return pl.BlockSpec((None,) + shape, lambda b, h: (h, 0, 0))

    return pl.pallas_call(
        _retention_kernel,
        grid=(T // SEQ, RET_HEADS),
        in_specs=[zspec(_COL_RQ), zspec(_COL_RK), zspec(_COL_RV), zspec(_COL_RG),
                  pl.BlockSpec((1, hd), lambda b, h: (0, h)),
                  tspec((RET_CHUNK, RET_CHUNK)), tspec((RET_CHUNK, hd)), tspec((RET_CHUNK, hd)),
                  tspec((1, hd))],
        out_specs=pl.BlockSpec((SEQ, hd), lambda b, h: (b, h)),
        out_shape=jax.ShapeDtypeStruct((T, RET_HEADS * hd), BF16),
        scratch_shapes=[pltpu.VMEM((hd, hd), F32)],
        compiler_params=_cparams(("arbitrary", "arbitrary")),
        name="retention",
    )(z, z, z, z, ret_norm_w.reshape(1, RET_HEADS * hd), dec, qd, kd, cd)


def _attention_kernel(q0, k0, v0, q1, k1, v1, q2, k2, v2, g_ref, o_ref,
                      on_ref, ln_ref, s_ref, p_ref, m_ref):
    blk = ATT_BLOCK
    hd = ATT_HEAD_DIM
    n_blk = SEQ // blk
    scale = hd ** -0.5
    neg_inf = jnp.float32(-jnp.inf)
    row = lax.broadcasted_iota(jnp.int32, (blk, 2 * blk), 0)
    col = lax.broadcasted_iota(jnp.int32, (blk, 2 * blk), 1)
    in_prev = col < blk
    valid = jnp.where(in_prev, -1, 1) * (row - jnp.where(in_prev, col, col - blk)) >= 0
    bias = jnp.where(valid, jnp.float32(0.0), neg_inf)
    contract_last = (((1,), (1,)), ((), ()))

    for g, (d, q_ref, k_ref, v_ref) in enumerate(
            zip(ATT_DILATIONS, (q0, q1, q2), (k0, k1, k2), (v0, v1, v2))):
        per_seq = n_blk // d
        for bi in range(n_blk):
            q = q_ref[bi * blk:(bi + 1) * blk, :]
            if bi % per_seq == 0:
                s_ref[bi, :, :blk] = jnp.full((blk, blk), neg_inf, F32)
                s_ref[bi, :, blk:] = lax.dot_general(q, k_ref[bi * blk:(bi + 1) * blk, :], contract_last,
                                                     preferred_element_type=F32)
            else:
                s_ref[bi] = lax.dot_general(q, k_ref[(bi - 1) * blk:(bi + 1) * blk, :], contract_last,
                                            preferred_element_type=F32)
        s = s_ref[...] * scale + bias[None]
        m = jnp.max(s, axis=-1, keepdims=True)
        p_ref[...] = jnp.exp(s - m).astype(BF16)
        m_ref[...] = jnp.broadcast_to(m, (n_blk, blk, hd))
        for bi in range(n_blk):
            first = bi % per_seq == 0
            vv = v_ref[(bi if first else bi - 1) * blk:(bi + 1) * blk, :]
            v_aug = jnp.concatenate([vv, jnp.ones_like(vv)], axis=1)
            pv = jnp.dot(p_ref[bi, :, blk:] if first else p_ref[bi], v_aug, preferred_element_type=F32)
            den = pv[:, hd:]
            r, i = divmod(bi, per_seq)
            dst = pl.ds(i * blk * d + r, blk, stride=d) if d > 1 else pl.ds(i * blk, blk)
            on_ref[g, dst, :] = pv[:, :hd] / den
            ln_ref[g, dst, :] = m_ref[bi] + jnp.log(den)

    l0, l1, l2 = ln_ref[0], ln_ref[1], ln_ref[2]
    m = jnp.maximum(jnp.maximum(l0, l1), l2)
    e0, e1, e2 = jnp.exp(l0 - m), jnp.exp(l1 - m), jnp.exp(l2 - m)
    a = (e0 * on_ref[0] + e1 * on_ref[1] + e2 * on_ref[2]) / (e0 + e1 + e2)
    gate = g_ref[...].astype(F32)
    o_ref[...] = (a * (gate * jax.nn.sigmoid(gate))).astype(BF16)


def _attention(z):
    T = z.shape[0]
    hd = ATT_HEAD_DIM
    per_tile = _PROJ_TN // hd

    def zspec(col_tile):
        return pl.BlockSpec((SEQ, hd), lambda b, h: (b, col_tile * per_tile + h))

    in_specs = []
    for g in range(len(ATT_DILATIONS)):
        in_specs += [zspec(_COL_AQ + g), zspec(_COL_AK + g), zspec(_COL_AV + g)]
    in_specs.append(zspec(_COL_AG))
    n_g = len(ATT_DILATIONS)
    return pl.pallas_call(
        _attention_kernel,
        grid=(T // SEQ, ATT_HEADS_PER_GROUP),
        in_specs=in_specs,
        out_specs=pl.BlockSpec((SEQ, hd), lambda b, h: (b, h)),
        out_shape=jax.ShapeDtypeStruct((T, ATT_HEADS_PER_GROUP * hd), BF16),
        scratch_shapes=[pltpu.VMEM((n_g, SEQ, hd), F32), pltpu.VMEM((n_g, SEQ, hd), F32),
                        pltpu.VMEM((SEQ // ATT_BLOCK, ATT_BLOCK, 2 * ATT_BLOCK), F32),
                        pltpu.VMEM((SEQ // ATT_BLOCK, ATT_BLOCK, 2 * ATT_BLOCK), BF16),
                        pltpu.VMEM((SEQ // ATT_BLOCK, ATT_BLOCK, hd), F32)],
        compiler_params=_cparams(("arbitrary", "arbitrary")),
        name="attention",
    )(*([z] * 10))


_CONV_ROWS = 512
_CONV_HALO = 32
_CONV_NORM_ROWS = 128


def _conv_kernel(cl_ref, cs_ref, g_ref, dw_ref, dwb_ref, lnw_ref, lnb_ref, o_ref, pad_ref, acc_ref):
    R, H = _CONV_ROWS, _CONV_HALO
    n_slab = D_MODEL // LANES

    @pl.when(pl.program_id(1) == 0)
    def _():
        pad_ref[:, 0:H, :] = jnp.zeros((n_slab, H, LANES), F32)

    lead = H - (CONV_TAPS - 1)
    for c in range(n_slab):
        cols = slice(c * LANES, (c + 1) * LANES)
        pad_ref[c, H:H + R, :] = cl_ref[:, cols].astype(F32) * cs_ref[:, cols].astype(F32)
    for c in range(n_slab):
        cols = slice(c * LANES, (c + 1) * LANES)
        acc = pad_ref[c, lead:lead + R, :] * dw_ref[0:1, cols]
        for k in range(1, CONV_TAPS):
            acc = acc + pad_ref[c, lead + k:lead + k + R, :] * dw_ref[k:k + 1, cols]
        acc_ref[:, cols] = acc
    for c in range(n_slab):
        pad_ref[c, 0:H, :] = pad_ref[c, R:R + H, :]

    def norm_rows(i, carry):
        rows = pl.ds(pl.multiple_of(i * _CONV_NORM_ROWS, _CONV_NORM_ROWS), _CONV_NORM_ROWS)
        c = acc_ref[rows, :] + dwb_ref[...]
        mu = jnp.mean(c, axis=-1, keepdims=True)
        cc = c - mu
        var = jnp.mean(cc * cc, axis=-1, keepdims=True)
        y = cc * lax.rsqrt(var + HEAD_NORM_EPS) * lnw_ref[...] + lnb_ref[...]
        y = y * jax.nn.sigmoid(y)
        o_ref[rows, :] = (y * g_ref[rows, :].astype(F32)).astype(BF16)
        return carry

    lax.fori_loop(0, R // _CONV_NORM_ROWS, norm_rows, 0)


def _conv(z, dw_w, dw_b, ln_w, ln_b):
    T = z.shape[0]
    R = _CONV_ROWS
    n_r = SEQ // R

    def zspec(col_tile):
        return pl.BlockSpec((R, D_MODEL), lambda b, j: (b * n_r + j, col_tile))

    def vec():
        return pl.BlockSpec((1, D_MODEL), lambda b, j: (0, 0))

    return pl.pallas_call(
        _conv_kernel,
        grid=(T // SEQ, n_r),
        in_specs=[zspec(_COL_CLIN), zspec(_COL_CGATE), zspec(_COL_CG),
                  pl.BlockSpec((CONV_TAPS, D_MODEL), lambda b, j: (0, 0)), vec(), vec(), vec()],
        out_specs=pl.BlockSpec((R, D_MODEL), lambda b, j: (b * n_r + j, 0)),
        out_shape=jax.ShapeDtypeStruct((T, D_MODEL), BF16),
        scratch_shapes=[pltpu.VMEM((D_MODEL // LANES, R + _CONV_HALO, LANES), F32),
                        pltpu.VMEM((R, D_MODEL), F32)],
        compiler_params=_cparams(("arbitrary", "arbitrary")),
        name="conv",
    )(z, z, z, dw_w, dw_b.reshape(1, D_MODEL), ln_w.reshape(1, D_MODEL), ln_b.reshape(1, D_MODEL))


_OUT_TM = 512


def _out_proj_kernel(final, r_ref, a_ref, c_ref, g0_ref, g1_ref, g2_ref, x_ref,
                     wr_ref, wa_ref, wc_ref, wo_ref, fw_ref, o_ref):
    def branch(in_ref, w_ref, gate_ref):
        y = jnp.dot(in_ref[...], w_ref[...], preferred_element_type=F32)
        return jax.nn.sigmoid(gate_ref[...].astype(F32)) * y

    merged = branch(r_ref, wr_ref, g0_ref) + branch(a_ref, wa_ref, g1_ref) + branch(c_ref, wc_ref, g2_ref)
    out = x_ref[...] + jnp.dot(merged.astype(BF16), wo_ref[...], preferred_element_type=F32)
    if final:
        out = out * lax.rsqrt(jnp.mean(out * out, axis=-1, keepdims=True) + NORM_EPS) * fw_ref[...]
    o_ref[...] = out


def _out_proj(rg, ag, cg, z, xf, w_r, w_a, w_c, w_o, final_w, final):
    T = xf.shape[0]
    tm = _OUT_TM

    def rows(col_tile=0):
        return pl.BlockSpec((tm, D_MODEL), lambda i: (i, col_tile))

    def weight():
        return pl.BlockSpec((D_MODEL, D_MODEL), lambda i: (0, 0))

    return pl.pallas_call(
        functools.partial(_out_proj_kernel, final),
        grid=(T // tm,),
        in_specs=[rows(), rows(), rows(), rows(_COL_MG), rows(_COL_MG + 1), rows(_COL_MG + 2), rows(),
                  weight(), weight(), weight(), weight(),
                  pl.BlockSpec((1, D_MODEL), lambda i: (0, 0))],
        out_specs=rows(),
        out_shape=jax.ShapeDtypeStruct((T, D_MODEL), F32),
        compiler_params=_cparams(("arbitrary",)),
        name="out_proj_final" if final else "out_proj",
    )(rg, ag, cg, z, z, z, xf, w_r, w_a, w_c, w_o, final_w.reshape(1, D_MODEL))


def kernel(x, norm_w, w_in, b_in, ret_norm_w, ret_w_o, att_w_o, conv_dw_w, conv_dw_b,
           conv_norm_w, conv_norm_b, conv_w_o, w_out, final_norm_w):
    B, S, D = x.shape
    assert (S, D) == (SEQ, D_MODEL) and w_in.shape[-1] == IN_WIDTH
    depth = w_in.shape[0]
    t256, t128 = _rope_tables()
    ret_tables = _retention_tables()
    xf = x.reshape(B * S, D)
    for l in range(depth):
        hs = _norm_perm(xf, norm_w[l])
        z = _in_proj(hs, w_in, l, b_in[l], t256, t128)
        rg = _retention(z, ret_norm_w[l], ret_tables)
        ag = _attention(z)
        cg = _conv(z, conv_dw_w[l], conv_dw_b[l], conv_norm_w[l], conv_norm_b[l])
        xf = _out_proj(rg, ag, cg, z, xf,
                       ret_w_o[l].astype(BF16), att_w_o[l].astype(BF16), conv_w_o[l].astype(BF16),
                       w_out[l].astype(BF16), final_norm_w, final=(l == depth - 1))
    return xf.reshape(B, S, D)
```

```python
import functools

import numpy as np
import jax
import jax.numpy as jnp
from jax import lax
from jax.experimental import pallas as pl
from jax.experimental.pallas import tpu as pltpu

F32 = jnp.float32
BF16 = jnp.bfloat16

D_MODEL = 1024
SEQ = 2048
ROPE_THETA = 10000.0
NORM_EPS = 1e-6
HEAD_NORM_EPS = 1e-5
RET_HEADS = 4
RET_HEAD_DIM = 256
RET_CHUNK = 256
ATT_DILATIONS = (1, 4, 16)
ATT_HEADS_PER_GROUP = 8
ATT_HEAD_DIM = 128
ATT_BLOCK = 128
CONV_TAPS = 31
IN_WIDTH = 20480

LANES = 128
SUBLANES = 8
VMEM_LIMIT = 56 * 1024 * 1024

_COL_RQ, _COL_RK, _COL_RV, _COL_RG = 0, 1, 2, 3
_COL_AQ, _COL_AK, _COL_AV, _COL_AG = 4, 7, 10, 13
_COL_CLIN, _COL_CGATE, _COL_CG, _COL_MG = 14, 15, 16, 17

_EPI_ROPE256, _EPI_NONE, _EPI_SILU, _EPI_SIGMOID, _EPI_ROPE128 = 0, 1, 2, 3, 4

_TILE_PLAN = (
    (_COL_RQ, 0, _EPI_ROPE256, 0), (_COL_RK, 0, _EPI_ROPE256, 1),
    (_COL_RV, 0, _EPI_NONE, 0), (_COL_RG, 0, _EPI_NONE, 0),
    (_COL_AQ, 0, _EPI_ROPE128, 0), (_COL_AK, 0, _EPI_ROPE128, 0), (_COL_AV, 0, _EPI_NONE, 0),
    (_COL_AG, 0, _EPI_NONE, 0), (_COL_CLIN, 0, _EPI_NONE, 0), (_COL_CGATE, 0, _EPI_SIGMOID, 0),
    (_COL_CG, 0, _EPI_SILU, 0),
    (_COL_MG, 0, _EPI_NONE, 0), (_COL_MG + 1, 0, _EPI_NONE, 0), (_COL_MG + 2, 0, _EPI_NONE, 0),
    (_COL_AQ + 1, 1, _EPI_ROPE128, 0), (_COL_AK + 1, 1, _EPI_ROPE128, 0), (_COL_AV + 1, 1, _EPI_NONE, 0),
    (_COL_AQ + 2, 2, _EPI_ROPE128, 0), (_COL_AK + 2, 2, _EPI_ROPE128, 0), (_COL_AV + 2, 2, _EPI_NONE, 0),
)


def _cparams(sem):
    return pltpu.CompilerParams(dimension_semantics=sem, vmem_limit_bytes=VMEM_LIMIT)


def _norm_perm_kernel(x_ref, w_ref, hs_ref, slab_ref):
    x = x_ref[...]
    h = x * lax.rsqrt(jnp.mean(x * x, axis=-1, keepdims=True) + NORM_EPS) * w_ref[...]
    hs_ref[0] = h.astype(BF16)
    n_slab = D_MODEL // LANES
    for c in range(n_slab):
        slab_ref[c] = h[:, c * LANES:(c + 1) * LANES]
    for li, d in enumerate(ATT_DILATIONS[1:], start=1):
        L = SEQ // d
        for r in range(d):
            for c in range(n_slab):
                hs_ref[li, r * L:(r + 1) * L, c * LANES:(c + 1) * LANES] = (
                    slab_ref[c, pl.ds(r, L, stride=d), :].astype(BF16))


def _norm_perm(xf, norm_w):
    T = xf.shape[0]
    return pl.pallas_call(
        _norm_perm_kernel,
        grid=(T // SEQ,),
        in_specs=[pl.BlockSpec((SEQ, D_MODEL), lambda b: (b, 0)),
                  pl.BlockSpec((1, D_MODEL), lambda b: (0, 0))],
        out_specs=pl.BlockSpec((3, SEQ, D_MODEL), lambda b: (0, b, 0)),
        out_shape=jax.ShapeDtypeStruct((3, T, D_MODEL), BF16),
        scratch_shapes=[pltpu.VMEM((D_MODEL // LANES, SEQ, LANES), F32)],
        compiler_params=_cparams(("arbitrary",)),
        name="norm_perm",
    )(xf, norm_w.reshape(1, D_MODEL))


_PROJ_TM = 4096
_PROJ_TN = 1024
_PROJ_NC = 256
_PROJ_MC = 256


def _in_proj_kernel(tbl_ref, hs_ref, w_ref, b_ref, t256_ref, t128_ref, o_ref):
    kind = tbl_ref[2, pl.program_id(1)]

    def emit(epilogue):
        for nc in range(_PROJ_TN // _PROJ_NC):
            cols = slice(nc * _PROJ_NC, (nc + 1) * _PROJ_NC)
            w = w_ref[:, cols].astype(BF16)
            for mc in range(_PROJ_TM // _PROJ_MC):
                rows = slice(mc * _PROJ_MC, (mc + 1) * _PROJ_MC)
                pos = slice(rows.start % SEQ, rows.start % SEQ + _PROJ_MC)
                z = jnp.dot(hs_ref[rows, :], w, preferred_element_type=F32) + b_ref[:, cols]
                o_ref[rows, cols] = epilogue(z, pos).astype(BF16)

    def rope256(z, rows):
        cos, sin = t256_ref[0, rows, :], t256_ref[1, rows, :]
        x1, x2 = z[:, :LANES], z[:, LANES:]
        return jnp.concatenate([x1 * cos - x2 * sin, x2 * cos + x1 * sin], axis=1)

    def rope128(z, rows):
        cos_full, sin_signed = t128_ref[0, rows, :], t128_ref[1, rows, :]
        halves = []
        for hh in range(_PROJ_NC // LANES):
            x = z[:, hh * LANES:(hh + 1) * LANES]
            halves.append(x * cos_full + pltpu.roll(x, LANES // 2, 1) * sin_signed)
        return jnp.concatenate(halves, axis=1)

    epilogues = {
        _EPI_ROPE256: rope256,
        _EPI_NONE: lambda z, rows: z,
        _EPI_SILU: lambda z, rows: z * jax.nn.sigmoid(z),
        _EPI_SIGMOID: lambda z, rows: jax.nn.sigmoid(z),
        _EPI_ROPE128: rope128,
    }
    for k, fn in epilogues.items():
        pl.when(kind == k)(functools.partial(emit, fn))


def _in_proj(hs, w_in, layer, b_in, t256, t128):
    T = hs.shape[1]
    tbl = jnp.asarray(np.array(_TILE_PLAN, dtype=np.int32).T)
    assert _PROJ_TM % SEQ == 0 and SEQ % _PROJ_MC == 0
    grid_spec = pltpu.PrefetchScalarGridSpec(
        num_scalar_prefetch=1,
        grid=(T // _PROJ_TM, IN_WIDTH // _PROJ_TN),
        in_specs=[
            pl.BlockSpec((None, _PROJ_TM, D_MODEL), lambda i, j, t: (t[1, j], i, 0)),
            pl.BlockSpec((None, D_MODEL, _PROJ_TN), lambda i, j, t: (layer, 0, t[0, j])),
            pl.BlockSpec((1, _PROJ_TN), lambda i, j, t: (0, t[0, j])),
            pl.BlockSpec((None, 2, SEQ, LANES), lambda i, j, t: (t[3, j], 0, 0, 0)),
            pl.BlockSpec((None, 2, SEQ, LANES), lambda i, j, t: (t[1, j], 0, 0, 0)),
        ],
        out_specs=pl.BlockSpec((_PROJ_TM, _PROJ_TN), lambda i, j, t: (i, t[0, j])),
    )
    return pl.pallas_call(
        _in_proj_kernel,
        grid_spec=grid_spec,
        out_shape=jax.ShapeDtypeStruct((T, IN_WIDTH), BF16),
        compiler_params=_cparams(("arbitrary", "arbitrary")),
        name="in_proj",
    )(tbl, hs, w_in, b_in.reshape(1, IN_WIDTH), t256, t128)


def _rope_tables():
    pos = jnp.arange(SEQ, dtype=jnp.int32)

    def cos_sin(hd, p):
        inv = ROPE_THETA ** (-jnp.arange(0, hd, 2, dtype=F32) / hd)
        ang = p.astype(F32)[:, None] * inv[None, :]
        return jnp.cos(ang), jnp.sin(ang)

    c, s = cos_sin(RET_HEAD_DIM, pos)
    k_scale = RET_HEAD_DIM ** -0.5
    t256 = jnp.stack([jnp.stack([c, s]), jnp.stack([c * k_scale, s * k_scale])])
    layouts = []
    for d in ATT_DILATIONS:
        p = pos.reshape(SEQ // d, d).T.reshape(SEQ)
        c, s = cos_sin(ATT_HEAD_DIM, p)
        layouts.append(jnp.stack([jnp.concatenate([c, c], axis=1), jnp.concatenate([-s, s], axis=1)]))
    return t256, jnp.stack(layouts)


def _retention_kernel(q_ref, k_ref, v_ref, g_ref, nw_ref, dec_ref, qd_ref, kd_ref, cd_ref, o_ref, st_ref):
    C = RET_CHUNK
    st_ref[...] = jnp.zeros_like(st_ref)
    dec = dec_ref[...]
    qd = qd_ref[...]
    kd = kd_ref[...]
    cd = cd_ref[...]
    nw = nw_ref[...]
    for n in range(SEQ // C):
        rows = slice(n * C, (n + 1) * C)
        q = q_ref[rows, :]
        k = k_ref[rows, :]
        v = v_ref[rows, :]
        s = lax.dot_general(q, k, (((1,), (1,)), ((), ())), preferred_element_type=F32) * dec
        intra = jnp.dot(s.astype(BF16), v, preferred_element_type=F32)
        state = st_ref[...]
        inter = jnp.dot(q, state.astype(BF16), preferred_element_type=F32) * qd
        kdec = (k.astype(F32) * kd).astype(BF16)
        kv = lax.dot_general(kdec, v, (((0,), (0,)), ((), ())), preferred_element_type=F32)
        st_ref[...] = state * cd + kv
        r = intra + inter
        mu = jnp.mean(r, axis=-1, keepdims=True)
        rc = r - mu
        var = jnp.mean(rc * rc, axis=-1, keepdims=True)
        rn = rc * lax.rsqrt(var + HEAD_NORM_EPS)
        gate = g_ref[rows, :].astype(F32)
        o_ref[rows, :] = (rn * nw * (gate * jax.nn.sigmoid(gate))).astype(BF16)


def _retention_tables():
    C = RET_CHUNK
    lg = jnp.log(1.0 - 2.0 ** (-5.0 - jnp.arange(RET_HEADS, dtype=F32)))
    idx = jnp.arange(C, dtype=F32)
    diff = idx[:, None] - idx[None, :]
    intra = jnp.where(diff[None] >= 0, jnp.exp(jnp.maximum(diff, 0.0)[None] * lg[:, None, None]), 0.0)
    q_decay = jnp.exp((idx + 1.0)[None, :] * lg[:, None])
    k_decay = jnp.exp((C - 1.0 - idx)[None, :] * lg[:, None])
    chunk_decay = jnp.exp(C * lg)
    bcast = (RET_HEADS, C, RET_HEAD_DIM)
    return (intra.astype(F32),
            jnp.broadcast_to(q_decay[:, :, None], bcast).astype(F32),
            jnp.broadcast_to(k_decay[:, :, None], bcast).astype(F32),
            jnp.broadcast_to(chunk_decay[:, None, None], (RET_HEADS, 1, RET_HEAD_DIM)).astype(F32))


def _retention(z, ret_norm_w, tables):
    T = z.shape[0]
    hd = RET_HEAD_DIM
    per_head = RET_HEADS
    dec, qd, kd, cd = tables

    def zspec(col_tile):
        return pl.BlockSpec((SEQ, hd), lambda b, h: (b, col_tile * per_head + h))

    def tspec(shape):
        return pl.BlockSpec((None,) + shape, lambda b, h: (h, 0, 0))

    return pl.pallas_call(
        _retention_kernel,
        grid=(T // SEQ, RET_HEADS),
        in_specs=[zspec(_COL_RQ), zspec(_COL_RK), zspec(_COL_RV), zspec(_COL_RG),
                  pl.BlockSpec((1, hd), lambda b, h: (0, h)),
                  tspec((RET_CHUNK, RET_CHUNK)), tspec((RET_CHUNK, hd)), tspec((RET_CHUNK, hd)),
                  tspec((1, hd))],
        out_specs=pl.BlockSpec((SEQ, hd), lambda b, h: (b, h)),
        out_shape=jax.ShapeDtypeStruct((T, RET_HEADS * hd), BF16),
        scratch_shapes=[pltpu.VMEM((hd, hd), F32)],
        compiler_params=_cparams(("arbitrary", "arbitrary")),
        name="retention",
    )(z, z, z, z, ret_norm_w.reshape(1, RET_HEADS * hd), dec, qd, kd, cd)


def _attention_kernel(q0, k0, v0, q1, k1, v1, q2, k2, v2, g_ref, o_ref,
                      n2_ref, d2_ref, m2_ref, n12_ref, d12_ref, m12_ref, s_ref, p_ref, m_ref):
    blk = ATT_BLOCK
    hd = ATT_HEAD_DIM
    n_blk = SEQ // blk
    scale = hd ** -0.5
    neg_inf = jnp.float32(-jnp.inf)
    row = lax.broadcasted_iota(jnp.int32, (blk, 2 * blk), 0)
    col = lax.broadcasted_iota(jnp.int32, (blk, 2 * blk), 1)
    in_prev = col < blk
    valid = jnp.where(in_prev, -1, 1) * (row - jnp.where(in_prev, col, col - blk)) >= 0
    bias = jnp.where(valid, jnp.float32(0.0), neg_inf)
    contract_last = (((1,), (1,)), ((), ()))

    def group(d, q_ref, k_ref, v_ref, finish):
        per_seq = n_blk // d
        for bi in range(n_blk):
            q = q_ref[bi * blk:(bi + 1) * blk, :]
            if bi % per_seq == 0:
                s_ref[bi, :, :blk] = jnp.full((blk, blk), neg_inf, F32)
                s_ref[bi, :, blk:] = lax.dot_general(q, k_ref[bi * blk:(bi + 1) * blk, :], contract_last,
                                                     preferred_element_type=F32)
            else:
                s_ref[bi] = lax.dot_general(q, k_ref[(bi - 1) * blk:(bi + 1) * blk, :], contract_last,
                                            preferred_element_type=F32)
        s = s_ref[...] * scale + bias[None]
        m = jnp.max(s, axis=-1, keepdims=True)
        p_ref[...] = jnp.exp(s - m).astype(BF16)
        m_ref[...] = jnp.broadcast_to(m, (n_blk, blk, hd))
        for bi in range(n_blk):
            first = bi % per_seq == 0
            vv = v_ref[(bi if first else bi - 1) * blk:(bi + 1) * blk, :]
            v_aug = jnp.concatenate([vv, jnp.ones_like(vv)], axis=1)
            pv = jnp.dot(p_ref[bi, :, blk:] if first else p_ref[bi], v_aug, preferred_element_type=F32)
            finish(bi, pv[:, :hd], pv[:, hd:], m_ref[bi])

    d1, d4, d16 = ATT_DILATIONS
    assert (d1, d4 * d4) == (1, d16)
    rows_by4 = SEQ // d4

    def finish_d16(bi, num, den, m):
        dst = pl.ds((bi % d4) * rows_by4 + bi // d4, blk, stride=d4)
        n2_ref[dst, :] = num
        d2_ref[dst, :] = den
        m2_ref[dst, :] = m

    def finish_d4(bi, num, den, m):
        rows = slice(bi * blk, (bi + 1) * blk)
        m2 = m2_ref[rows, :]
        m12 = jnp.maximum(m, m2)
        w1, w2 = jnp.exp(m - m12), jnp.exp(m2 - m12)
        r, i = divmod(bi, rows_by4 // blk)
        dst = pl.ds(i * blk * d4 + r, blk, stride=d4)
        n12_ref[dst, :] = w1 * num + w2 * n2_ref[rows, :]
        d12_ref[dst, :] = w1 * den + w2 * d2_ref[rows, :]
        m12_ref[dst, :] = m12

    def finish_d1(bi, num, den, m):
        rows = slice(bi * blk, (bi + 1) * blk)
        m12 = m12_ref[rows, :]
        m_all = jnp.maximum(m, m12)
        w0, w12 = jnp.exp(m - m_all), jnp.exp(m12 - m_all)
        a = (w0 * num + w12 * n12_ref[rows, :]) / (w0 * den + w12 * d12_ref[rows, :])
        gate = g_ref[rows, :].astype(F32)
        o_ref[rows, :] = (a * (gate * jax.nn.sigmoid(gate))).astype(BF16)

    group(d16, q2, k2, v2, finish_d16)
    group(d4, q1, k1, v1, finish_d4)
    group(d1, q0, k0, v0, finish_d1)


def _attention(z):
    T = z.shape[0]
    hd = ATT_HEAD_DIM
    per_tile = _PROJ_TN // hd

    def zspec(col_tile):
        return pl.BlockSpec((SEQ, hd), lambda b, h: (b, col_tile * per_tile + h))

    in_specs = []
    for g in range(len(ATT_DILATIONS)):
        in_specs += [zspec(_COL_AQ + g), zspec(_COL_AK + g), zspec(_COL_AV + g)]
    in_specs.append(zspec(_COL_AG))
    return pl.pallas_call(
        _attention_kernel,
        grid=(T // SEQ, ATT_HEADS_PER_GROUP),
        in_specs=in_specs,
        out_specs=pl.BlockSpec((SEQ, hd), lambda b, h: (b, h)),
        out_shape=jax.ShapeDtypeStruct((T, ATT_HEADS_PER_GROUP * hd), BF16),
        scratch_shapes=[pltpu.VMEM((SEQ, hd), F32)] * 6 + [
                        pltpu.VMEM((SEQ // ATT_BLOCK, ATT_BLOCK, 2 * ATT_BLOCK), F32),
                        pltpu.VMEM((SEQ // ATT_BLOCK, ATT_BLOCK, 2 * ATT_BLOCK), BF16),
                        pltpu.VMEM((SEQ // ATT_BLOCK, ATT_BLOCK, hd), F32)],
        compiler_params=_cparams(("arbitrary", "arbitrary")),
        name="attention",
    )(*([z] * 10))


_CONV_ROWS = 512
_CONV_HALO = 32
_CONV_NORM_ROWS = 128


def _conv_kernel(cl_ref, cs_ref, g_ref, dw_ref, dwb_ref, lnw_ref, lnb_ref, o_ref, pad_ref, acc_ref):
    R, H = _CONV_ROWS, _CONV_HALO
    n_slab = D_MODEL // LANES

    @pl.when(pl.program_id(1) == 0)
    def _():
        pad_ref[:, 0:H, :] = jnp.zeros((n_slab, H, LANES), F32)

    lead = H - (CONV_TAPS - 1)
    for c in range(n_slab):
        cols = slice(c * LANES, (c + 1) * LANES)
        pad_ref[c, H:H + R, :] = cl_ref[:, cols].astype(F32) * cs_ref[:, cols].astype(F32)
    for c in range(n_slab):
        cols = slice(c * LANES, (c + 1) * LANES)
        acc = pad_ref[c, lead:lead + R, :] * dw_ref[0:1, cols]
        for k in range(1, CONV_TAPS):
            acc = acc + pad_ref[c, lead + k:lead + k + R, :] * dw_ref[k:k + 1, cols]
        acc_ref[:, cols] = acc
    for c in range(n_slab):
        pad_ref[c, 0:H, :] = pad_ref[c, R:R + H, :]

    def norm_rows(i, carry):
        rows = pl.ds(pl.multiple_of(i * _CONV_NORM_ROWS, _CONV_NORM_ROWS), _CONV_NORM_ROWS)
        c = acc_ref[rows, :] + dwb_ref[...]
        mu = jnp.mean(c, axis=-1, keepdims=True)
        cc = c - mu
        var = jnp.mean(cc * cc, axis=-1, keepdims=True)
        y = cc * lax.rsqrt(var + HEAD_NORM_EPS) * lnw_ref[...] + lnb_ref[...]
        y = y * jax.nn.sigmoid(y)
        o_ref[rows, :] = (y * g_ref[rows, :].astype(F32)).astype(BF16)
        return carry

    lax.fori_loop(0, R // _CONV_NORM_ROWS, norm_rows, 0)


def _conv(z, dw_w, dw_b, ln_w, ln_b):
    T = z.shape[0]
    R = _CONV_ROWS
    n_r = SEQ // R

    def zspec(col_tile):
        return pl.BlockSpec((R, D_MODEL), lambda b, j: (b * n_r + j, col_tile))

    def vec():
        return pl.BlockSpec((1, D_MODEL), lambda b, j: (0, 0))

    return pl.pallas_call(
        _conv_kernel,
        grid=(T // SEQ, n_r),
        in_specs=[zspec(_COL_CLIN), zspec(_COL_CGATE), zspec(_COL_CG),
                  pl.BlockSpec((CONV_TAPS, D_MODEL), lambda b, j: (0, 0)), vec(), vec(), vec()],
        out_specs=pl.BlockSpec((R, D_MODEL), lambda b, j: (b * n_r + j, 0)),
        out_shape=jax.ShapeDtypeStruct((T, D_MODEL), BF16),
        scratch_shapes=[pltpu.VMEM((D_MODEL // LANES, R + _CONV_HALO, LANES), F32),
                        pltpu.VMEM((R, D_MODEL), F32)],
        compiler_params=_cparams(("arbitrary", "arbitrary")),
        name="conv",
    )(z, z, z, dw_w, dw_b.reshape(1, D_MODEL), ln_w.reshape(1, D_MODEL), ln_b.reshape(1, D_MODEL))


_OUT_TM = 512


def _out_proj_kernel(final, r_ref, a_ref, c_ref, g0_ref, g1_ref, g2_ref, x_ref,
                     wr_ref, wa_ref, wc_ref, wo_ref, fw_ref, o_ref):
    def branch(in_ref, w_ref, gate_ref):
        y = jnp.dot(in_ref[...], w_ref[...], preferred_element_type=F32)
        return jax.nn.sigmoid(gate_ref[...].astype(F32)) * y

    merged = branch(r_ref, wr_ref, g0_ref) + branch(a_ref, wa_ref, g1_ref) + branch(c_ref, wc_ref, g2_ref)
    out = x_ref[...] + jnp.dot(merged.astype(BF16), wo_ref[...], preferred_element_type=F32)
    if final:
        out = out * lax.rsqrt(jnp.mean(out * out, axis=-1, keepdims=True) + NORM_EPS) * fw_ref[...]
    o_ref[...] = out


def _out_proj(rg, ag, cg, z, xf, w_r, w_a, w_c, w_o, final_w, final):
    T = xf.shape[0]
    tm = _OUT_TM

    def rows(col_tile=0):
        return pl.BlockSpec((tm, D_MODEL), lambda i: (i, col_tile))

    def weight():
        return pl.BlockSpec((D_MODEL, D_MODEL), lambda i: (0, 0))

    return pl.pallas_call(
        functools.partial(_out_proj_kernel, final),
        grid=(T // tm,),
        in_specs=[rows(), rows(), rows(), rows(_COL_MG), rows(_COL_MG + 1), rows(_COL_MG + 2), rows(),
                  weight(), weight(), weight(), weight(),
                  pl.BlockSpec((1, D_MODEL), lambda i: (0, 0))],
        out_specs=rows(),
        out_shape=jax.ShapeDtypeStruct((T, D_MODEL), F32),
        compiler_params=_cparams(("arbitrary",)),
        name="out_proj_final" if final else "out_proj",
    )(rg, ag, cg, z, z, z, xf, w_r, w_a, w_c, w_o, final_w.reshape(1, D_MODEL))


def kernel(x, norm_w, w_in, b_in, ret_norm_w, ret_w_o, att_w_o, conv_dw_w, conv_dw_b,
           conv_norm_w, conv_norm_b, conv_w_o, w_out, final_norm_w):
    B, S, D = x.shape
    assert (S, D) == (SEQ, D_MODEL) and w_in.shape[-1] == IN_WIDTH
    depth = w_in.shape[0]
    t256, t128 = _rope_tables()
    ret_tables = _retention_tables()
    xf = x.reshape(B * S, D)
    for l in range(depth):
        hs = _norm_perm(xf, norm_w[l])
        z = _in_proj(hs, w_in, l, b_in[l], t256, t128)
        rg = _retention(z, ret_norm_w[l], ret_tables)
        ag = _attention(z)
        cg = _conv(z, conv_dw_w[l], conv_dw_b[l], conv_norm_w[l], conv_norm_b[l])
        xf = _out_proj(rg, ag, cg, z, xf,
                       ret_w_o[l].astype(BF16), att_w_o[l].astype(BF16), conv_w_o[l].astype(BF16),
                       w_out[l].astype(BF16), final_norm_w, final=(l == depth - 1))
    return xf.reshape(B, S, D)
```

```python
import functools

import numpy as np
import jax
import jax.numpy as jnp
from jax import lax
from jax.experimental import pallas as pl
from jax.experimental.pallas import tpu as pltpu

F32 = jnp.float32
BF16 = jnp.bfloat16

D_MODEL = 1024
SEQ = 2048
ROPE_THETA = 10000.0
NORM_EPS = 1e-6
HEAD_NORM_EPS = 1e-5
RET_HEADS = 4
RET_HEAD_DIM = 256
RET_CHUNK = 256
ATT_DILATIONS = (1, 4, 16)
ATT_HEADS_PER_GROUP = 8
ATT_HEAD_DIM = 128
ATT_BLOCK = 128
CONV_TAPS = 31
IN_WIDTH = 20480

LANES = 128
SUBLANES = 8
VMEM_LIMIT = 56 * 1024 * 1024

_COL_RQ, _COL_RK, _COL_RV, _COL_RG = 0, 1, 2, 3
_COL_AQ, _COL_AK, _COL_AV, _COL_AG = 4, 7, 10, 13
_COL_CLIN, _COL_CGATE, _COL_CG, _COL_MG = 14, 15, 16, 17

_EPI_ROPE256, _EPI_NONE, _EPI_SILU, _EPI_SIGMOID, _EPI_ROPE128 = 0, 1, 2, 3, 4

_TILE_PLAN = (
    (_COL_RQ, 0, _EPI_ROPE256, 0), (_COL_RK, 0, _EPI_ROPE256, 1),
    (_COL_RV, 0, _EPI_NONE, 0), (_COL_RG, 0, _EPI_NONE, 0),
    (_COL_AQ, 0, _EPI_ROPE128, 0), (_COL_AK, 0, _EPI_ROPE128, 0), (_COL_AV, 0, _EPI_NONE, 0),
    (_COL_AG, 0, _EPI_NONE, 0), (_COL_CLIN, 0, _EPI_NONE, 0), (_COL_CGATE, 0, _EPI_SIGMOID, 0),
    (_COL_CG, 0, _EPI_SILU, 0),
    (_COL_MG, 0, _EPI_NONE, 0), (_COL_MG + 1, 0, _EPI_NONE, 0), (_COL_MG + 2, 0, _EPI_NONE, 0),
    (_COL_AQ + 1, 1, _EPI_ROPE128, 0), (_COL_AK + 1, 1, _EPI_ROPE128, 0), (_COL_AV + 1, 1, _EPI_NONE, 0),
    (_COL_AQ + 2, 2, _EPI_ROPE128, 0), (_COL_AK + 2, 2, _EPI_ROPE128, 0), (_COL_AV + 2, 2, _EPI_NONE, 0),
)


def _cparams(sem):
    return pltpu.CompilerParams(dimension_semantics=sem, vmem_limit_bytes=VMEM_LIMIT)


def _norm_perm_kernel(x_ref, w_ref, hs_ref, slab_ref):
    x = x_ref[...]
    h = x * lax.rsqrt(jnp.mean(x * x, axis=-1, keepdims=True) + NORM_EPS) * w_ref[...]
    hs_ref[0] = h.astype(BF16)
    n_slab = D_MODEL // LANES
    for c in range(n_slab):
        slab_ref[c] = h[:, c * LANES:(c + 1) * LANES]
    for li, d in enumerate(ATT_DILATIONS[1:], start=1):
        L = SEQ // d
        for r in range(d):
            for c in range(n_slab):
                hs_ref[li, r * L:(r + 1) * L, c * LANES:(c + 1) * LANES] = (
                    slab_ref[c, pl.ds(r, L, stride=d), :].astype(BF16))


def _norm_perm(xf, norm_w):
    T = xf.shape[0]
    return pl.pallas_call(
        _norm_perm_kernel,
        grid=(T // SEQ,),
        in_specs=[pl.BlockSpec((SEQ, D_MODEL), lambda b: (b, 0)),
                  pl.BlockSpec((1, D_MODEL), lambda b: (0, 0))],
        out_specs=pl.BlockSpec((3, SEQ, D_MODEL), lambda b: (0, b, 0)),
        out_shape=jax.ShapeDtypeStruct((3, T, D_MODEL), BF16),
        scratch_shapes=[pltpu.VMEM((D_MODEL // LANES, SEQ, LANES), F32)],
        compiler_params=_cparams(("arbitrary",)),
        name="norm_perm",
    )(xf, norm_w.reshape(1, D_MODEL))


_PROJ_TM = 4096
_PROJ_TN = 1024
_PROJ_NC = 256
_PROJ_MC = 256


def _in_proj_kernel(tbl_ref, hs_ref, w_ref, b_ref, t256_ref, t128_ref, o_ref):
    kind = tbl_ref[2, pl.program_id(1)]

    def emit(epilogue):
        def one_batch(bt, carry):
            base = pl.multiple_of(bt * SEQ, SEQ)
            for nc in range(_PROJ_TN // _PROJ_NC):
                cols = slice(nc * _PROJ_NC, (nc + 1) * _PROJ_NC)
                w = w_ref[:, cols].astype(BF16)
                for mc in range(SEQ // _PROJ_MC):
                    pos = slice(mc * _PROJ_MC, (mc + 1) * _PROJ_MC)
                    rows = pl.ds(base + mc * _PROJ_MC, _PROJ_MC)
                    z = jnp.dot(hs_ref[rows, :], w, preferred_element_type=F32) + b_ref[:, cols]
                    o_ref[rows, cols] = epilogue(z, pos).astype(BF16)
            return carry

        lax.fori_loop(0, _PROJ_TM // SEQ, one_batch, 0)

    def rope256(z, rows):
        cos, sin = t256_ref[0, rows, :], t256_ref[1, rows, :]
        x1, x2 = z[:, :LANES], z[:, LANES:]
        return jnp.concatenate([x1 * cos - x2 * sin, x2 * cos + x1 * sin], axis=1)

    def rope128(z, rows):
        cos_full, sin_signed = t128_ref[0, rows, :], t128_ref[1, rows, :]
        halves = []
        for hh in range(_PROJ_NC // LANES):
            x = z[:, hh * LANES:(hh + 1) * LANES]
            halves.append(x * cos_full + pltpu.roll(x, LANES // 2, 1) * sin_signed)
        return jnp.concatenate(halves, axis=1)

    epilogues = {
        _EPI_ROPE256: rope256,
        _EPI_NONE: lambda z, rows: z,
        _EPI_SILU: lambda z, rows: z * jax.nn.sigmoid(z),
        _EPI_SIGMOID: lambda z, rows: jax.nn.sigmoid(z),
        _EPI_ROPE128: rope128,
    }
    for k, fn in epilogues.items():
        pl.when(kind == k)(functools.partial(emit, fn))


def _in_proj(hs, w_in, layer, b_in, t256, t128):
    T = hs.shape[1]
    tbl = jnp.asarray(np.array(_TILE_PLAN, dtype=np.int32).T)
    assert _PROJ_TM % SEQ == 0 and SEQ % _PROJ_MC == 0
    grid_spec = pltpu.PrefetchScalarGridSpec(
        num_scalar_prefetch=1,
        grid=(T // _PROJ_TM, IN_WIDTH // _PROJ_TN),
        in_specs=[
            pl.BlockSpec((None, _PROJ_TM, D_MODEL), lambda i, j, t: (t[1, j], i, 0)),
            pl.BlockSpec((None, D_MODEL, _PROJ_TN), lambda i, j, t: (layer, 0, t[0, j])),
            pl.BlockSpec((1, _PROJ_TN), lambda i, j, t: (0, t[0, j])),
            pl.BlockSpec((None, 2, SEQ, LANES), lambda i, j, t: (t[3, j], 0, 0, 0)),
            pl.BlockSpec((None, 2, SEQ, LANES), lambda i, j, t: (t[1, j], 0, 0, 0)),
        ],
        out_specs=pl.BlockSpec((_PROJ_TM, _PROJ_TN), lambda i, j, t: (i, t[0, j])),
    )
    return pl.pallas_call(
        _in_proj_kernel,
        grid_spec=grid_spec,
        out_shape=jax.ShapeDtypeStruct((T, IN_WIDTH), BF16),
        compiler_params=_cparams(("arbitrary", "arbitrary")),
        name="in_proj",
    )(tbl, hs, w_in, b_in.reshape(1, IN_WIDTH), t256, t128)


def _rope_tables():
    pos = jnp.arange(SEQ, dtype=jnp.int32)

    def cos_sin(hd, p):
        inv = ROPE_THETA ** (-jnp.arange(0, hd, 2, dtype=F32) / hd)
        ang = p.astype(F32)[:, None] * inv[None, :]
        return jnp.cos(ang), jnp.sin(ang)

    c, s = cos_sin(RET_HEAD_DIM, pos)
    k_scale = RET_HEAD_DIM ** -0.5
    t256 = jnp.stack([jnp.stack([c, s]), jnp.stack([c * k_scale, s * k_scale])])
    layouts = []
    for d in ATT_DILATIONS:
        p = pos.reshape(SEQ // d, d).T.reshape(SEQ)
        c, s = cos_sin(ATT_HEAD_DIM, p)
        layouts.append(jnp.stack([jnp.concatenate([c, c], axis=1), jnp.concatenate([-s, s], axis=1)]))
    return t256, jnp.stack(layouts)


def _retention_kernel(q_ref, k_ref, v_ref, g_ref, nw_ref, dec_ref, qd_ref, kd_ref, cd_ref, o_ref, st_ref):
    C = RET_CHUNK
    st_ref[...] = jnp.zeros_like(st_ref)
    dec = dec_ref[...]
    qd = qd_ref[...]
    kd = kd_ref[...]
    cd = cd_ref[...]
    nw = nw_ref[...]
    for n in range(SEQ // C):
        rows = slice(n * C, (n + 1) * C)
        q = q_ref[rows, :]
        k = k_ref[rows, :]
        v = v_ref[rows, :]
        s = lax.dot_general(q, k, (((1,), (1,)), ((), ())), preferred_element_type=F32) * dec
        intra = jnp.dot(s.astype(BF16), v, preferred_element_type=F32)
        state = st_ref[...]
        inter = jnp.dot(q, state.astype(BF16), preferred_element_type=F32) * qd
        kdec = (k.astype(F32) * kd).astype(BF16)
        kv = lax.dot_general(kdec, v, (((0,), (0,)), ((), ())), preferred_element_type=F32)
        st_ref[...] = state * cd + kv
        r = intra + inter
        mu = jnp.mean(r, axis=-1, keepdims=True)
        rc = r - mu
        var = jnp.mean(rc * rc, axis=-1, keepdims=True)
        rn = rc * lax.rsqrt(var + HEAD_NORM_EPS)
        gate = g_ref[rows, :].astype(F32)
        o_ref[rows, :] = (rn * nw * (gate * jax.nn.sigmoid(gate))).astype(BF16)


def _retention_tables():
    C = RET_CHUNK
    lg = jnp.log(1.0 - 2.0 ** (-5.0 - jnp.arange(RET_HEADS, dtype=F32)))
    idx = jnp.arange(C, dtype=F32)
    diff = idx[:, None] - idx[None, :]
    intra = jnp.where(diff[None] >= 0, jnp.exp(jnp.maximum(diff, 0.0)[None] * lg[:, None, None]), 0.0)
    q_decay = jnp.exp((idx + 1.0)[None, :] * lg[:, None])
    k_decay = jnp.exp((C - 1.0 - idx)[None, :] * lg[:, None])
    chunk_decay = jnp.exp(C * lg)
    bcast = (RET_HEADS, C, RET_HEAD_DIM)
    return (intra.astype(F32),
            jnp.broadcast_to(q_decay[:, :, None], bcast).astype(F32),
            jnp.broadcast_to(k_decay[:, :, None], bcast).astype(F32),
            jnp.broadcast_to(chunk_decay[:, None, None], (RET_HEADS, 1, RET_HEAD_DIM)).astype(F32))


def _retention(z, ret_norm_w, tables):
    T = z.shape[0]
    hd = RET_HEAD_DIM
    per_head = RET_HEADS
    dec, qd, kd, cd = tables

    def zspec(col_tile):
        return pl.BlockSpec((SEQ, hd), lambda b, h: (b, col_tile * per_head + h))

    def tspec(shape):
        return pl.BlockSpec((None,) + shape, lambda b, h: (h, 0, 0))

    return pl.pallas_call(
        _retention_kernel,
        grid=(T // SEQ, RET_HEADS),
        in_specs=[zspec(_COL_RQ), zspec(_COL_RK), zspec(_COL_RV), zspec(_COL_RG),
                  pl.BlockSpec((1, hd), lambda b, h: (0, h)),
                  tspec((RET_CHUNK, RET_CHUNK)), tspec((RET_CHUNK, hd)), tspec((RET_CHUNK, hd)),
                  tspec((1, hd))],
        out_specs=pl.BlockSpec((SEQ, hd), lambda b, h: (b, h)),
        out_shape=jax.ShapeDtypeStruct((T, RET_HEADS * hd), BF16),
        scratch_shapes=[pltpu.VMEM((hd, hd), F32)],
        compiler_params=_cparams(("arbitrary", "arbitrary")),
        name="retention",
    )(z, z, z, z, ret_norm_w.reshape(1, RET_HEADS * hd), dec, qd, kd, cd)


def _attention_kernel(q0, k0, v0, q1, k1, v1, q2, k2, v2, g_ref, o_ref,
                      n2_ref, d2_ref, m2_ref, n12_ref, d12_ref, m12_ref, s_ref, p_ref, m_ref):
    blk = ATT_BLOCK
    hd = ATT_HEAD_DIM
    n_blk = SEQ // blk
    scale = hd ** -0.5
    neg_inf = jnp.float32(-jnp.inf)
    row = lax.broadcasted_iota(jnp.int32, (blk, 2 * blk), 0)
    col = lax.broadcasted_iota(jnp.int32, (blk, 2 * blk), 1)
    in_prev = col < blk
    valid = jnp.where(in_prev, -1, 1) * (row - jnp.where(in_prev, col, col - blk)) >= 0
    bias = jnp.where(valid, jnp.float32(0.0), neg_inf)
    contract_last = (((1,), (1,)), ((), ()))

    def group(d, q_ref, k_ref, v_ref, finish):
        per_seq = n_blk // d
        for bi in range(n_blk):
            q = q_ref[bi * blk:(bi + 1) * blk, :]
            if bi % per_seq == 0:
                s_ref[bi, :, :blk] = jnp.full((blk, blk), neg_inf, F32)
                s_ref[bi, :, blk:] = lax.dot_general(q, k_ref[bi * blk:(bi + 1) * blk, :], contract_last,
                                                     preferred_element_type=F32)
            else:
                s_ref[bi] = lax.dot_general(q, k_ref[(bi - 1) * blk:(bi + 1) * blk, :], contract_last,
                                            preferred_element_type=F32)
        s = s_ref[...] * scale + bias[None]
        m = jnp.max(s, axis=-1, keepdims=True)
        p_ref[...] = jnp.exp(s - m).astype(BF16)
        m_ref[...] = jnp.broadcast_to(m, (n_blk, blk, hd))
        for bi in range(n_blk):
            first = bi % per_seq == 0
            vv = v_ref[(bi if first else bi - 1) * blk:(bi + 1) * blk, :]
            v_aug = jnp.concatenate([vv, jnp.ones_like(vv)], axis=1)
            pv = jnp.dot(p_ref[bi, :, blk:] if first else p_ref[bi], v_aug, preferred_element_type=F32)
            finish(bi, pv[:, :hd], pv[:, hd:], m_ref[bi])

    d1, d4, d16 = ATT_DILATIONS
    assert (d1, d4 * d4) == (1, d16)
    rows_by4 = SEQ // d4

    def finish_d16(bi, num, den, m):
        dst = pl.ds((bi % d4) * rows_by4 + bi // d4, blk, stride=d4)
        n2_ref[dst, :] = num
        d2_ref[dst, :] = den
        m2_ref[dst, :] = m

    def finish_d4(bi, num, den, m):
        rows = slice(bi * blk, (bi + 1) * blk)
        m2 = m2_ref[rows, :]
        m12 = jnp.maximum(m, m2)
        w1, w2 = jnp.exp(m - m12), jnp.exp(m2 - m12)
        r, i = divmod(bi, rows_by4 // blk)
        dst = pl.ds(i * blk * d4 + r, blk, stride=d4)
        n12_ref[dst, :] = w1 * num + w2 * n2_ref[rows, :]
        d12_ref[dst, :] = w1 * den + w2 * d2_ref[rows, :]
        m12_ref[dst, :] = m12

    def finish_d1(bi, num, den, m):
        rows = slice(bi * blk, (bi + 1) * blk)
        m12 = m12_ref[rows, :]
        m_all = jnp.maximum(m, m12)
        w0, w12 = jnp.exp(m - m_all), jnp.exp(m12 - m_all)
        a = (w0 * num + w12 * n12_ref[rows, :]) / (w0 * den + w12 * d12_ref[rows, :])
        gate = g_ref[rows, :].astype(F32)
        o_ref[rows, :] = (a * (gate * jax.nn.sigmoid(gate))).astype(BF16)

    group(d16, q2, k2, v2, finish_d16)
    group(d4, q1, k1, v1, finish_d4)
    group(d1, q0, k0, v0, finish_d1)


def _attention(z):
    T = z.shape[0]
    hd = ATT_HEAD_DIM
    per_tile = _PROJ_TN // hd

    def zspec(col_tile):
        return pl.BlockSpec((SEQ, hd), lambda b, h: (b, col_tile * per_tile + h))

    in_specs = []
    for g in range(len(ATT_DILATIONS)):
        in_specs += [zspec(_COL_AQ + g), zspec(_COL_AK + g), zspec(_COL_AV + g)]
    in_specs.append(zspec(_COL_AG))
    return pl.pallas_call(
        _attention_kernel,
        grid=(T // SEQ, ATT_HEADS_PER_GROUP),
        in_specs=in_specs,
        out_specs=pl.BlockSpec((SEQ, hd), lambda b, h: (b, h)),
        out_shape=jax.ShapeDtypeStruct((T, ATT_HEADS_PER_GROUP * hd), BF16),
        scratch_shapes=[pltpu.VMEM((SEQ, hd), F32)] * 6 + [
                        pltpu.VMEM((SEQ // ATT_BLOCK, ATT_BLOCK, 2 * ATT_BLOCK), F32),
                        pltpu.VMEM((SEQ // ATT_BLOCK, ATT_BLOCK, 2 * ATT_BLOCK), BF16),
                        pltpu.VMEM((SEQ // ATT_BLOCK, ATT_BLOCK, hd), F32)],
        compiler_params=_cparams(("arbitrary", "arbitrary")),
        name="attention",
    )(*([z] * 10))


_CONV_ROWS = 512
_CONV_HALO = 32
_CONV_NORM_ROWS = 128


def _conv_kernel(cl_ref, cs_ref, g_ref, dw_ref, dwb_ref, lnw_ref, lnb_ref, o_ref, pad_ref, acc_ref):
    R, H = _CONV_ROWS, _CONV_HALO
    n_slab = D_MODEL // LANES

    @pl.when(pl.program_id(1) == 0)
    def _():
        pad_ref[:, 0:H, :] = jnp.zeros((n_slab, H, LANES), F32)

    lead = H - (CONV_TAPS - 1)
    for c in range(n_slab):
        cols = slice(c * LANES, (c + 1) * LANES)
        pad_ref[c, H:H + R, :] = cl_ref[:, cols].astype(F32) * cs_ref[:, cols].astype(F32)
    for c in range(n_slab):
        cols = slice(c * LANES, (c + 1) * LANES)
        acc = pad_ref[c, lead:lead + R, :] * dw_ref[0:1, cols]
        for k in range(1, CONV_TAPS):
            acc = acc + pad_ref[c, lead + k:lead + k + R, :] * dw_ref[k:k + 1, cols]
        acc_ref[:, cols] = acc
    for c in range(n_slab):
        pad_ref[c, 0:H, :] = pad_ref[c, R:R + H, :]

    def norm_rows(i, carry):
        rows = pl.ds(pl.multiple_of(i * _CONV_NORM_ROWS, _CONV_NORM_ROWS), _CONV_NORM_ROWS)
        c = acc_ref[rows, :] + dwb_ref[...]
        mu = jnp.mean(c, axis=-1, keepdims=True)
        cc = c - mu
        var = jnp.mean(cc * cc, axis=-1, keepdims=True)
        y = cc * lax.rsqrt(var + HEAD_NORM_EPS) * lnw_ref[...] + lnb_ref[...]
        y = y * jax.nn.sigmoid(y)
        o_ref[rows, :] = (y * g_ref[rows, :].astype(F32)).astype(BF16)
        return carry

    lax.fori_loop(0, R // _CONV_NORM_ROWS, norm_rows, 0)


def _conv(z, dw_w, dw_b, ln_w, ln_b):
    T = z.shape[0]
    R = _CONV_ROWS
    n_r = SEQ // R

    def zspec(col_tile):
        return pl.BlockSpec((R, D_MODEL), lambda b, j: (b * n_r + j, col_tile))

    def vec():
        return pl.BlockSpec((1, D_MODEL), lambda b, j: (0, 0))

    return pl.pallas_call(
        _conv_kernel,
        grid=(T // SEQ, n_r),
        in_specs=[zspec(_COL_CLIN), zspec(_COL_CGATE), zspec(_COL_CG),
                  pl.BlockSpec((CONV_TAPS, D_MODEL), lambda b, j: (0, 0)), vec(), vec(), vec()],
        out_specs=pl.BlockSpec((R, D_MODEL), lambda b, j: (b * n_r + j, 0)),
        out_shape=jax.ShapeDtypeStruct((T, D_MODEL), BF16),
        scratch_shapes=[pltpu.VMEM((D_MODEL // LANES, R + _CONV_HALO, LANES), F32),
                        pltpu.VMEM((R, D_MODEL), F32)],
        compiler_params=_cparams(("arbitrary", "arbitrary")),
        name="conv",
    )(z, z, z, dw_w, dw_b.reshape(1, D_MODEL), ln_w.reshape(1, D_MODEL), ln_b.reshape(1, D_MODEL))


_OUT_TM = 512


def _out_proj_kernel(final, r_ref, a_ref, c_ref, g0_ref, g1_ref, g2_ref, x_ref,
                     wr_ref, wa_ref, wc_ref, wo_ref, fw_ref, o_ref):
    def branch(in_ref, w_ref, gate_ref):
        y = jnp.dot(in_ref[...], w_ref[...], preferred_element_type=F32)
        return jax.nn.sigmoid(gate_ref[...].astype(F32)) * y

    merged = branch(r_ref, wr_ref, g0_ref) + branch(a_ref, wa_ref, g1_ref) + branch(c_ref, wc_ref, g2_ref)
    out = x_ref[...] + jnp.dot(merged.astype(BF16), wo_ref[...], preferred_element_type=F32)
    if final:
        out = out * lax.rsqrt(jnp.mean(out * out, axis=-1, keepdims=True) + NORM_EPS) * fw_ref[...]
    o_ref[...] = out


def _out_proj(rg, ag, cg, z, xf, w_r, w_a, w_c, w_o, final_w, final):
    T = xf.shape[0]
    tm = _OUT_TM

    def rows(col_tile=0):
        return pl.BlockSpec((tm, D_MODEL), lambda i: (i, col_tile))

    def weight():
        return pl.BlockSpec((D_MODEL, D_MODEL), lambda i: (0, 0))

    return pl.pallas_call(
        functools.partial(_out_proj_kernel, final),
        grid=(T // tm,),
        in_specs=[rows(), rows(), rows(), rows(_COL_MG), rows(_COL_MG + 1), rows(_COL_MG + 2), rows(),
                  weight(), weight(), weight(), weight(),
                  pl.BlockSpec((1, D_MODEL), lambda i: (0, 0))],
        out_specs=rows(),
        out_shape=jax.ShapeDtypeStruct((T, D_MODEL), F32),
        compiler_params=_cparams(("arbitrary",)),
        name="out_proj_final" if final else "out_proj",
    )(rg, ag, cg, z, z, z, xf, w_r, w_a, w_c, w_o, final_w.reshape(1, D_MODEL))


def kernel(x, norm_w, w_in, b_in, ret_norm_w, ret_w_o, att_w_o, conv_dw_w, conv_dw_b,
           conv_norm_w, conv_norm_b, conv_w_o, w_out, final_norm_w):
    B, S, D = x.shape
    assert (S, D) == (SEQ, D_MODEL) and w_in.shape[-1] == IN_WIDTH
    depth = w_in.shape[0]
    t256, t128 = _rope_tables()
    ret_tables = _retention_tables()
    xf = x.reshape(B * S, D)
    for l in range(depth):
        hs = _norm_perm(xf, norm_w[l])
        z = _in_proj(hs, w_in, l, b_in[l], t256, t128)
        rg = _retention(z, ret_norm_w[l], ret_tables)
        ag = _attention(z)
        cg = _conv(z, conv_dw_w[l], conv_dw_b[l], conv_norm_w[l], conv_norm_b[l])
        xf = _out_proj(rg, ag, cg, z, xf,
                       ret_w_o[l].astype(BF16), att_w_o[l].astype(BF16), conv_w_o[l].astype(BF16),
                       w_out[l].astype(BF16), final_norm_w, final=(l == depth - 1))
    return xf.reshape(B, S, D)
```

```python
import functools

import numpy as np
import jax
import jax.numpy as jnp
from jax import lax
from jax.experimental import pallas as pl
from jax.experimental.pallas import tpu as pltpu

F32 = jnp.float32
BF16 = jnp.bfloat16

D_MODEL = 1024
SEQ = 2048
ROPE_THETA = 10000.0
NORM_EPS = 1e-6
HEAD_NORM_EPS = 1e-5
RET_HEADS = 4
RET_HEAD_DIM = 256
RET_CHUNK = 256
ATT_DILATIONS = (1, 4, 16)
ATT_HEADS_PER_GROUP = 8
ATT_HEAD_DIM = 128
ATT_BLOCK = 128
CONV_TAPS = 31
IN_WIDTH = 20480

LANES = 128
SUBLANES = 8
VMEM_LIMIT = 56 * 1024 * 1024

_COL_RQ, _COL_RK, _COL_RV, _COL_RG = 0, 1, 2, 3
_COL_AQ, _COL_AK, _COL_AV, _COL_AG = 4, 7, 10, 13
_COL_CLIN, _COL_CGATE, _COL_CG, _COL_MG = 14, 15, 16, 17

_EPI_ROPE256, _EPI_NONE, _EPI_SILU, _EPI_SIGMOID, _EPI_ROPE128 = 0, 1, 2, 3, 4

_TILE_PLAN = (
    (_COL_RQ, 0, _EPI_ROPE256, 0), (_COL_RK, 0, _EPI_ROPE256, 1),
    (_COL_RV, 0, _EPI_NONE, 0), (_COL_RG, 0, _EPI_NONE, 0),
    (_COL_AQ, 0, _EPI_ROPE128, 0), (_COL_AK, 0, _EPI_ROPE128, 0), (_COL_AV, 0, _EPI_NONE, 0),
    (_COL_AG, 0, _EPI_NONE, 0), (_COL_CLIN, 0, _EPI_NONE, 0), (_COL_CGATE, 0, _EPI_SIGMOID, 0),
    (_COL_CG, 0, _EPI_SILU, 0),
    (_COL_MG, 0, _EPI_NONE, 0), (_COL_MG + 1, 0, _EPI_NONE, 0), (_COL_MG + 2, 0, _EPI_NONE, 0),
    (_COL_AQ + 1, 1, _EPI_ROPE128, 0), (_COL_AK + 1, 1, _EPI_ROPE128, 0), (_COL_AV + 1, 1, _EPI_NONE, 0),
    (_COL_AQ + 2, 2, _EPI_ROPE128, 0), (_COL_AK + 2, 2, _EPI_ROPE128, 0), (_COL_AV + 2, 2, _EPI_NONE, 0),
)


def _cparams(sem):
    return pltpu.CompilerParams(dimension_semantics=sem, vmem_limit_bytes=VMEM_LIMIT)


def _norm_perm_kernel(x_ref, w_ref, hs_ref, slab_ref):
    x = x_ref[...]
    h = x * lax.rsqrt(jnp.mean(x * x, axis=-1, keepdims=True) + NORM_EPS) * w_ref[...]
    hs_ref[0] = h.astype(BF16)
    n_slab = D_MODEL // LANES
    for c in range(n_slab):
        slab_ref[c] = h[:, c * LANES:(c + 1) * LANES]
    for li, d in enumerate(ATT_DILATIONS[1:], start=1):
        L = SEQ // d
        for r in range(d):
            for c in range(n_slab):
                hs_ref[li, r * L:(r + 1) * L, c * LANES:(c + 1) * LANES] = (
                    slab_ref[c, pl.ds(r, L, stride=d), :].astype(BF16))


def _norm_perm(xf, norm_w):
    T = xf.shape[0]
    return pl.pallas_call(
        _norm_perm_kernel,
        grid=(T // SEQ,),
        in_specs=[pl.BlockSpec((SEQ, D_MODEL), lambda b: (b, 0)),
                  pl.BlockSpec((1, D_MODEL), lambda b: (0, 0))],
        out_specs=pl.BlockSpec((3, SEQ, D_MODEL), lambda b: (0, b, 0)),
        out_shape=jax.ShapeDtypeStruct((3, T, D_MODEL), BF16),
        scratch_shapes=[pltpu.VMEM((D_MODEL // LANES, SEQ, LANES), F32)],
        compiler_params=_cparams(("arbitrary",)),
        name="norm_perm",
    )(xf, norm_w.reshape(1, D_MODEL))


_PROJ_TM = 4096
_PROJ_TN = 1024
_PROJ_NC = 256
_PROJ_MC = 2048


def _in_proj_kernel(tbl_ref, hs_ref, w_ref, b_ref, t256_ref, t128_ref, o_ref):
    kind = tbl_ref[2, pl.program_id(1)]

    def emit(epilogue):
        def one_batch(bt, carry):
            base = pl.multiple_of(bt * SEQ, SEQ)
            for nc in range(_PROJ_TN // _PROJ_NC):
                cols = slice(nc * _PROJ_NC, (nc + 1) * _PROJ_NC)
                w = w_ref[:, cols].astype(BF16)
                for mc in range(SEQ // _PROJ_MC):
                    pos = slice(mc * _PROJ_MC, (mc + 1) * _PROJ_MC)
                    rows = pl.ds(base + mc * _PROJ_MC, _PROJ_MC)
                    z = jnp.dot(hs_ref[rows, :], w, preferred_element_type=F32) + b_ref[:, cols]
                    o_ref[rows, cols] = epilogue(z, pos).astype(BF16)
            return carry

        lax.fori_loop(0, _PROJ_TM // SEQ, one_batch, 0)

    def rope256(z, rows):
        cos, sin = t256_ref[0, rows, :], t256_ref[1, rows, :]
        parts = []
        for hh in range(_PROJ_NC // RET_HEAD_DIM):
            x1 = z[:, hh * RET_HEAD_DIM:hh * RET_HEAD_DIM + LANES]
            x2 = z[:, hh * RET_HEAD_DIM + LANES:(hh + 1) * RET_HEAD_DIM]
            parts += [x1 * cos - x2 * sin, x2 * cos + x1 * sin]
        return jnp.concatenate(parts, axis=1)

    def rope128(z, rows):
        cos_full, sin_signed = t128_ref[0, rows, :], t128_ref[1, rows, :]
        halves = []
        for hh in range(_PROJ_NC // LANES):
            x = z[:, hh * LANES:(hh + 1) * LANES]
            halves.append(x * cos_full + pltpu.roll(x, LANES // 2, 1) * sin_signed)
        return jnp.concatenate(halves, axis=1)

    epilogues = {
        _EPI_ROPE256: rope256,
        _EPI_NONE: lambda z, rows: z,
        _EPI_SILU: lambda z, rows: z * jax.nn.sigmoid(z),
        _EPI_SIGMOID: lambda z, rows: jax.nn.sigmoid(z),
        _EPI_ROPE128: rope128,
    }
    for k, fn in epilogues.items():
        pl.when(kind == k)(functools.partial(emit, fn))


def _in_proj(hs, w_in, layer, b_in, t256, t128):
    T = hs.shape[1]
    tbl = jnp.asarray(np.array(_TILE_PLAN, dtype=np.int32).T)
    assert _PROJ_TM % SEQ == 0 and SEQ % _PROJ_MC == 0
    grid_spec = pltpu.PrefetchScalarGridSpec(
        num_scalar_prefetch=1,
        grid=(T // _PROJ_TM, IN_WIDTH // _PROJ_TN),
        in_specs=[
            pl.BlockSpec((None, _PROJ_TM, D_MODEL), lambda i, j, t: (t[1, j], i, 0)),
            pl.BlockSpec((None, D_MODEL, _PROJ_TN), lambda i, j, t: (layer, 0, t[0, j])),
            pl.BlockSpec((1, _PROJ_TN), lambda i, j, t: (0, t[0, j])),
            pl.BlockSpec((None, 2, SEQ, LANES), lambda i, j, t: (t[3, j], 0, 0, 0)),
            pl.BlockSpec((None, 2, SEQ, LANES), lambda i, j, t: (t[1, j], 0, 0, 0)),
        ],
        out_specs=pl.BlockSpec((_PROJ_TM, _PROJ_TN), lambda i, j, t: (i, t[0, j])),
    )
    return pl.pallas_call(
        _in_proj_kernel,
        grid_spec=grid_spec,
        out_shape=jax.ShapeDtypeStruct((T, IN_WIDTH), BF16),
        compiler_params=_cparams(("arbitrary", "arbitrary")),
        name="in_proj",
    )(tbl, hs, w_in, b_in.reshape(1, IN_WIDTH), t256, t128)


def _rope_tables():
    pos = jnp.arange(SEQ, dtype=jnp.int32)

    def cos_sin(hd, p):
        inv = ROPE_THETA ** (-jnp.arange(0, hd, 2, dtype=F32) / hd)
        ang = p.astype(F32)[:, None] * inv[None, :]
        return jnp.cos(ang), jnp.sin(ang)

    c, s = cos_sin(RET_HEAD_DIM, pos)
    k_scale = RET_HEAD_DIM ** -0.5
    t256 = jnp.stack([jnp.stack([c, s]), jnp.stack([c * k_scale, s * k_scale])])
    layouts = []
    for d in ATT_DILATIONS:
        p = pos.reshape(SEQ // d, d).T.reshape(SEQ)
        c, s = cos_sin(ATT_HEAD_DIM, p)
        layouts.append(jnp.stack([jnp.concatenate([c, c], axis=1), jnp.concatenate([-s, s], axis=1)]))
    return t256, jnp.stack(layouts)


def _retention_kernel(q_ref, k_ref, v_ref, g_ref, nw_ref, dec_ref, qd_ref, kd_ref, cd_ref, o_ref, st_ref):
    C = RET_CHUNK
    st_ref[...] = jnp.zeros_like(st_ref)
    dec = dec_ref[...]
    qd = qd_ref[...]
    kd = kd_ref[...]
    cd = cd_ref[...]
    nw = nw_ref[...]
    for n in range(SEQ // C):
        rows = slice(n * C, (n + 1) * C)
        q = q_ref[rows, :]
        k = k_ref[rows, :]
        v = v_ref[rows, :]
        s = lax.dot_general(q, k, (((1,), (1,)), ((), ())), preferred_element_type=F32) * dec
        intra = jnp.dot(s.astype(BF16), v, preferred_element_type=F32)
        state = st_ref[...]
        inter = jnp.dot(q, state.astype(BF16), preferred_element_type=F32) * qd
        kdec = (k.astype(F32) * kd).astype(BF16)
        kv = lax.dot_general(kdec, v, (((0,), (0,)), ((), ())), preferred_element_type=F32)
        st_ref[...] = state * cd + kv
        r = intra + inter
        mu = jnp.mean(r, axis=-1, keepdims=True)
        rc = r - mu
        var = jnp.mean(rc * rc, axis=-1, keepdims=True)
        rn = rc * lax.rsqrt(var + HEAD_NORM_EPS)
        gate = g_ref[rows, :].astype(F32)
        o_ref[rows, :] = (rn * nw * (gate * jax.nn.sigmoid(gate))).astype(BF16)


def _retention_tables():
    C = RET_CHUNK
    lg = jnp.log(1.0 - 2.0 ** (-5.0 - jnp.arange(RET_HEADS, dtype=F32)))
    idx = jnp.arange(C, dtype=F32)
    diff = idx[:, None] - idx[None, :]
    intra = jnp.where(diff[None] >= 0, jnp.exp(jnp.maximum(diff, 0.0)[None] * lg[:, None, None]), 0.0)
    q_decay = jnp.exp((idx + 1.0)[None, :] * lg[:, None])
    k_decay = jnp.exp((C - 1.0 - idx)[None, :] * lg[:, None])
    chunk_decay = jnp.exp(C * lg)
    bcast = (RET_HEADS, C, RET_HEAD_DIM)
    return (intra.astype(F32),
            jnp.broadcast_to(q_decay[:, :, None], bcast).astype(F32),
            jnp.broadcast_to(k_decay[:, :, None], bcast).astype(F32),
            jnp.broadcast_to(chunk_decay[:, None, None], (RET_HEADS, 1, RET_HEAD_DIM)).astype(F32))


def _retention(z, ret_norm_w, tables):
    T = z.shape[0]
    hd = RET_HEAD_DIM
    per_head = RET_HEADS
    dec, qd, kd, cd = tables

    def zspec(col_tile):
        return pl.BlockSpec((SEQ, hd), lambda b, h: (b, col_tile * per_head + h))

    def tspec(shape):
        return pl.BlockSpec((None,) + shape, lambda b, h: (h, 0, 0))

    return pl.pallas_call(
        _retention_kernel,
        grid=(T // SEQ, RET_HEADS),
        in_specs=[zspec(_COL_RQ), zspec(_COL_RK), zspec(_COL_RV), zspec(_COL_RG),
                  pl.BlockSpec((1, hd), lambda b, h: (0, h)),
                  tspec((RET_CHUNK, RET_CHUNK)), tspec((RET_CHUNK, hd)), tspec((RET_CHUNK, hd)),
                  tspec((1, hd))],
        out_specs=pl.BlockSpec((SEQ, hd), lambda b, h: (b, h)),
        out_shape=jax.ShapeDtypeStruct((T, RET_HEADS * hd), BF16),
        scratch_shapes=[pltpu.VMEM((hd, hd), F32)],
        compiler_params=_cparams(("arbitrary", "arbitrary")),
        name="retention",
    )(z, z, z, z, ret_norm_w.reshape(1, RET_HEADS * hd), dec, qd, kd, cd)


def _attention_kernel(q0, k0, v0, q1, k1, v1, q2, k2, v2, g_ref, o_ref,
                      n2_ref, d2_ref, m2_ref, n12_ref, d12_ref, m12_ref, s_ref, p_ref, m_ref):
    blk = ATT_BLOCK
    hd = ATT_HEAD_DIM
    n_blk = SEQ // blk
    scale = hd ** -0.5
    neg_inf = jnp.float32(-jnp.inf)
    row = lax.broadcasted_iota(jnp.int32, (blk, 2 * blk), 0)
    col = lax.broadcasted_iota(jnp.int32, (blk, 2 * blk), 1)
    in_prev = col < blk
    valid = jnp.where(in_prev, -1, 1) * (row - jnp.where(in_prev, col, col - blk)) >= 0
    bias = jnp.where(valid, jnp.float32(0.0), neg_inf)
    contract_last = (((1,), (1,)), ((), ()))

    def group(d, q_ref, k_ref, v_ref, finish):
        per_seq = n_blk // d
        for bi in range(n_blk):
            q = q_ref[bi * blk:(bi + 1) * blk, :]
            if bi % per_seq == 0:
                s_ref[bi, :, :blk] = jnp.full((blk, blk), neg_inf, F32)
                s_ref[bi, :, blk:] = lax.dot_general(q, k_ref[bi * blk:(bi + 1) * blk, :], contract_last,
                                                     preferred_element_type=F32)
            else:
                s_ref[bi] = lax.dot_general(q, k_ref[(bi - 1) * blk:(bi + 1) * blk, :], contract_last,
                                            preferred_element_type=F32)
        s = s_ref[...] * scale + bias[None]
        m = jnp.max(s, axis=-1, keepdims=True)
        p_ref[...] = jnp.exp(s - m).astype(BF16)
        m_ref[...] = jnp.broadcast_to(m, (n_blk, blk, hd))
        for bi in range(n_blk):
            first = bi % per_seq == 0
            vv = v_ref[(bi if first else bi - 1) * blk:(bi + 1) * blk, :]
            v_aug = jnp.concatenate([vv, jnp.ones_like(vv)], axis=1)
            pv = jnp.dot(p_ref[bi, :, blk:] if first else p_ref[bi], v_aug, preferred_element_type=F32)
            finish(bi, pv[:, :hd], pv[:, hd:], m_ref[bi])

    d1, d4, d16 = ATT_DILATIONS
    assert (d1, d4 * d4) == (1, d16)
    rows_by4 = SEQ // d4

    def finish_d16(bi, num, den, m):
        dst = pl.ds((bi % d4) * rows_by4 + bi // d4, blk, stride=d4)
        n2_ref[dst, :] = num
        d2_ref[dst, :] = den
        m2_ref[dst, :] = m

    def finish_d4(bi, num, den, m):
        rows = slice(bi * blk, (bi + 1) * blk)
        m2 = m2_ref[rows, :]
        m12 = jnp.maximum(m, m2)
        w1, w2 = jnp.exp(m - m12), jnp.exp(m2 - m12)
        r, i = divmod(bi, rows_by4 // blk)
        dst = pl.ds(i * blk * d4 + r, blk, stride=d4)
        n12_ref[dst, :] = w1 * num + w2 * n2_ref[rows, :]
        d12_ref[dst, :] = w1 * den + w2 * d2_ref[rows, :]
        m12_ref[dst, :] = m12

    def finish_d1(bi, num, den, m):
        rows = slice(bi * blk, (bi + 1) * blk)
        m12 = m12_ref[rows, :]
        m_all = jnp.maximum(m, m12)
        w0, w12 = jnp.exp(m - m_all), jnp.exp(m12 - m_all)
        a = (w0 * num + w12 * n12_ref[rows, :]) / (w0 * den + w12 * d12_ref[rows, :])
        gate = g_ref[rows, :].astype(F32)
        o_ref[rows, :] = (a * (gate * jax.nn.sigmoid(gate))).astype(BF16)

    group(d16, q2, k2, v2, finish_d16)
    group(d4, q1, k1, v1, finish_d4)
    group(d1, q0, k0, v0, finish_d1)


def _attention(z):
    T = z.shape[0]
    hd = ATT_HEAD_DIM
    per_tile = _PROJ_TN // hd

    def zspec(col_tile):
        return pl.BlockSpec((SEQ, hd), lambda b, h: (b, col_tile * per_tile + h))

    in_specs = []
    for g in range(len(ATT_DILATIONS)):
        in_specs += [zspec(_COL_AQ + g), zspec(_COL_AK + g), zspec(_COL_AV + g)]
    in_specs.append(zspec(_COL_AG))
    return pl.pallas_call(
        _attention_kernel,
        grid=(T // SEQ, ATT_HEADS_PER_GROUP),
        in_specs=in_specs,
        out_specs=pl.BlockSpec((SEQ, hd), lambda b, h: (b, h)),
        out_shape=jax.ShapeDtypeStruct((T, ATT_HEADS_PER_GROUP * hd), BF16),
        scratch_shapes=[pltpu.VMEM((SEQ, hd), F32)] * 6 + [
                        pltpu.VMEM((SEQ // ATT_BLOCK, ATT_BLOCK, 2 * ATT_BLOCK), F32),
                        pltpu.VMEM((SEQ // ATT_BLOCK, ATT_BLOCK, 2 * ATT_BLOCK), BF16),
                        pltpu.VMEM((SEQ // ATT_BLOCK, ATT_BLOCK, hd), F32)],
        compiler_params=_cparams(("arbitrary", "arbitrary")),
        name="attention",
    )(*([z] * 10))


_CONV_ROWS = 512
_CONV_HALO = 32
_CONV_NORM_ROWS = 128


def _conv_kernel(cl_ref, cs_ref, g_ref, dw_ref, dwb_ref, lnw_ref, lnb_ref, o_ref, pad_ref, acc_ref):
    R, H = _CONV_ROWS, _CONV_HALO
    n_slab = D_MODEL // LANES

    @pl.when(pl.program_id(1) == 0)
    def _():
        pad_ref[:, 0:H, :] = jnp.zeros((n_slab, H, LANES), F32)

    lead = H - (CONV_TAPS - 1)
    for c in range(n_slab):
        cols = slice(c * LANES, (c + 1) * LANES)
        pad_ref[c, H:H + R, :] = cl_ref[:, cols].astype(F32) * cs_ref[:, cols].astype(F32)
    for c in range(n_slab):
        cols = slice(c * LANES, (c + 1) * LANES)
        acc = pad_ref[c, lead:lead + R, :] * dw_ref[0:1, cols]
        for k in range(1, CONV_TAPS):
            acc = acc + pad_ref[c, lead + k:lead + k + R, :] * dw_ref[k:k + 1, cols]
        acc_ref[:, cols] = acc
    for c in range(n_slab):
        pad_ref[c, 0:H, :] = pad_ref[c, R:R + H, :]

    def norm_rows(i, carry):
        rows = pl.ds(pl.multiple_of(i * _CONV_NORM_ROWS, _CONV_NORM_ROWS), _CONV_NORM_ROWS)
        c = acc_ref[rows, :] + dwb_ref[...]
        mu = jnp.mean(c, axis=-1, keepdims=True)
        cc = c - mu
        var = jnp.mean(cc * cc, axis=-1, keepdims=True)
        y = cc * lax.rsqrt(var + HEAD_NORM_EPS) * lnw_ref[...] + lnb_ref[...]
        y = y * jax.nn.sigmoid(y)
        o_ref[rows, :] = (y * g_ref[rows, :].astype(F32)).astype(BF16)
        return carry

    lax.fori_loop(0, R // _CONV_NORM_ROWS, norm_rows, 0)


def _conv(z, dw_w, dw_b, ln_w, ln_b):
    T = z.shape[0]
    R = _CONV_ROWS
    n_r = SEQ // R

    def zspec(col_tile):
        return pl.BlockSpec((R, D_MODEL), lambda b, j: (b * n_r + j, col_tile))

    def vec():
        return pl.BlockSpec((1, D_MODEL), lambda b, j: (0, 0))

    return pl.pallas_call(
        _conv_kernel,
        grid=(T // SEQ, n_r),
        in_specs=[zspec(_COL_CLIN), zspec(_COL_CGATE), zspec(_COL_CG),
                  pl.BlockSpec((CONV_TAPS, D_MODEL), lambda b, j: (0, 0)), vec(), vec(), vec()],
        out_specs=pl.BlockSpec((R, D_MODEL), lambda b, j: (b * n_r + j, 0)),
        out_shape=jax.ShapeDtypeStruct((T, D_MODEL), BF16),
        scratch_shapes=[pltpu.VMEM((D_MODEL // LANES, R + _CONV_HALO, LANES), F32),
                        pltpu.VMEM((R, D_MODEL), F32)],
        compiler_params=_cparams(("arbitrary", "arbitrary")),
        name="conv",
    )(z, z, z, dw_w, dw_b.reshape(1, D_MODEL), ln_w.reshape(1, D_MODEL), ln_b.reshape(1, D_MODEL))


_OUT_TM = 512


def _out_proj_kernel(final, r_ref, a_ref, c_ref, g0_ref, g1_ref, g2_ref, x_ref,
                     wr_ref, wa_ref, wc_ref, wo_ref, fw_ref, o_ref):
    def branch(in_ref, w_ref, gate_ref):
        y = jnp.dot(in_ref[...], w_ref[...], preferred_element_type=F32)
        return jax.nn.sigmoid(gate_ref[...].astype(F32)) * y

    merged = branch(r_ref, wr_ref, g0_ref) + branch(a_ref, wa_ref, g1_ref) + branch(c_ref, wc_ref, g2_ref)
    out = x_ref[...] + jnp.dot(merged.astype(BF16), wo_ref[...], preferred_element_type=F32)
    if final:
        out = out * lax.rsqrt(jnp.mean(out * out, axis=-1, keepdims=True) + NORM_EPS) * fw_ref[...]
    o_ref[...] = out


def _out_proj(rg, ag, cg, z, xf, w_r, w_a, w_c, w_o, final_w, final):
    T = xf.shape[0]
    tm = _OUT_TM

    def rows(col_tile=0):
        return pl.BlockSpec((tm, D_MODEL), lambda i: (i, col_tile))

    def weight():
        return pl.BlockSpec((D_MODEL, D_MODEL), lambda i: (0, 0))

    return pl.pallas_call(
        functools.partial(_out_proj_kernel, final),
        grid=(T // tm,),
        in_specs=[rows(), rows(), rows(), rows(_COL_MG), rows(_COL_MG + 1), rows(_COL_MG + 2), rows(),
                  weight(), weight(), weight(), weight(),
                  pl.BlockSpec((1, D_MODEL), lambda i: (0, 0))],
        out_specs=rows(),
        out_shape=jax.ShapeDtypeStruct((T, D_MODEL), F32),
        compiler_params=_cparams(("arbitrary",)),
        name="out_proj_final" if final else "out_proj",
    )(rg, ag, cg, z, z, z, xf, w_r, w_a, w_c, w_o, final_w.reshape(1, D_MODEL))


def kernel(x, norm_w, w_in, b_in, ret_norm_w, ret_w_o, att_w_o, conv_dw_w, conv_dw_b,
           conv_norm_w, conv_norm_b, conv_w_o, w_out, final_norm_w):
    B, S, D = x.shape
    assert (S, D) == (SEQ, D_MODEL) and w_in.shape[-1] == IN_WIDTH
    depth = w_in.shape[0]
    t256, t128 = _rope_tables()
    ret_tables = _retention_tables()
    xf = x.reshape(B * S, D)
    for l in range(depth):
        hs = _norm_perm(xf, norm_w[l])
        z = _in_proj(hs, w_in, l, b_in[l], t256, t128)
        rg = _retention(z, ret_norm_w[l], ret_tables)
        ag = _attention(z)
        cg = _conv(z, conv_dw_w[l], conv_dw_b[l], conv_norm_w[l], conv_norm_b[l])
        xf = _out_proj(rg, ag, cg, z, xf,
                       ret_w_o[l].astype(BF16), att_w_o[l].astype(BF16), conv_w_o[l].astype(BF16),
                       w_out[l].astype(BF16), final_norm_w, final=(l == depth - 1))
    return xf.reshape(B, S, D)
```

```python
import functools

import numpy as np
import jax
import jax.numpy as jnp
from jax import lax
from jax.experimental import pallas as pl
from jax.experimental.pallas import tpu as pltpu

F32 = jnp.float32
BF16 = jnp.bfloat16

D_MODEL = 1024
SEQ = 2048
ROPE_THETA = 10000.0
NORM_EPS = 1e-6
HEAD_NORM_EPS = 1e-5
RET_HEADS = 4
RET_HEAD_DIM = 256
RET_CHUNK = 256
ATT_DILATIONS = (1, 4, 16)
ATT_HEADS_PER_GROUP = 8
ATT_HEAD_DIM = 128
ATT_BLOCK = 128
CONV_TAPS = 31
IN_WIDTH = 20480

LANES = 128
SUBLANES = 8
VMEM_LIMIT = 56 * 1024 * 1024

_COL_RQ, _COL_RK, _COL_RV, _COL_RG = 0, 1, 2, 3
_COL_AQ, _COL_AK, _COL_AV, _COL_AG = 4, 7, 10, 13
_COL_CLIN, _COL_CGATE, _COL_CG, _COL_MG = 14, 15, 16, 17

_EPI_ROPE256, _EPI_NONE, _EPI_SILU, _EPI_SIGMOID, _EPI_ROPE128 = 0, 1, 2, 3, 4

_TILE_PLAN = (
    (_COL_RQ, 0, _EPI_ROPE256, 0), (_COL_RK, 0, _EPI_ROPE256, 1),
    (_COL_RV, 0, _EPI_NONE, 0), (_COL_RG, 0, _EPI_NONE, 0),
    (_COL_AQ, 0, _EPI_ROPE128, 0), (_COL_AK, 0, _EPI_ROPE128, 0), (_COL_AV, 0, _EPI_NONE, 0),
    (_COL_AG, 0, _EPI_NONE, 0), (_COL_CLIN, 0, _EPI_NONE, 0), (_COL_CGATE, 0, _EPI_SIGMOID, 0),
    (_COL_CG, 0, _EPI_SILU, 0),
    (_COL_MG, 0, _EPI_NONE, 0), (_COL_MG + 1, 0, _EPI_NONE, 0), (_COL_MG + 2, 0, _EPI_NONE, 0),
    (_COL_AQ + 1, 1, _EPI_ROPE128, 0), (_COL_AK + 1, 1, _EPI_ROPE128, 0), (_COL_AV + 1, 1, _EPI_NONE, 0),
    (_COL_AQ + 2, 2, _EPI_ROPE128, 0), (_COL_AK + 2, 2, _EPI_ROPE128, 0), (_COL_AV + 2, 2, _EPI_NONE, 0),
)


def _cparams(sem):
    return pltpu.CompilerParams(dimension_semantics=sem, vmem_limit_bytes=VMEM_LIMIT)


_NORM_COL_SPLIT = 2


def _norm_perm_kernel(x_ref, w_ref, hs_ref, slab_ref, slab4_ref, inv_ref):
    d4, d16 = ATT_DILATIONS[1:]
    assert d4 * d4 == d16
    L4, L16 = SEQ // d4, SEQ // d16
    width = D_MODEL // _NORM_COL_SPLIT
    n_slab = width // LANES
    j = pl.program_id(1)

    @pl.when(j == 0)
    def _():
        x = x_ref[...]
        inv = lax.rsqrt(jnp.mean(x * x, axis=-1, keepdims=True) + NORM_EPS)
        inv_ref[...] = jnp.broadcast_to(inv, (SEQ, LANES))

    def columns(col0):
        inv = inv_ref[...]
        for c in range(n_slab):
            src = slice(col0 + c * LANES, col0 + (c + 1) * LANES)
            dst = slice(c * LANES, (c + 1) * LANES)
            h = x_ref[:, src] * inv * w_ref[:, src]
            slab_ref[c] = h
            hs_ref[0, :, dst] = h.astype(BF16)
        for r in range(d4):
            for c in range(n_slab):
                dst = slice(c * LANES, (c + 1) * LANES)
                v = slab_ref[c, pl.ds(r, L4, stride=d4), :]
                slab4_ref[c, r * L4:(r + 1) * L4, :] = v
                hs_ref[1, r * L4:(r + 1) * L4, dst] = v.astype(BF16)
        for r in range(d16):
            for c in range(n_slab):
                dst = slice(c * LANES, (c + 1) * LANES)
                v = slab4_ref[c, pl.ds((r % d4) * L4 + r // d4, L16, stride=d4), :]
                hs_ref[2, r * L16:(r + 1) * L16, dst] = v.astype(BF16)

    for half in range(_NORM_COL_SPLIT):
        pl.when(j == half)(functools.partial(columns, half * width))


def _norm_perm(xf, norm_w):
    T = xf.shape[0]
    width = D_MODEL // _NORM_COL_SPLIT
    slab = pltpu.VMEM((width // LANES, SEQ, LANES), F32)
    return pl.pallas_call(
        _norm_perm_kernel,
        grid=(T // SEQ, _NORM_COL_SPLIT),
        in_specs=[pl.BlockSpec((SEQ, D_MODEL), lambda b, j: (b, 0)),
                  pl.BlockSpec((1, D_MODEL), lambda b, j: (0, 0))],
        out_specs=pl.BlockSpec((3, SEQ, width), lambda b, j: (0, b, j)),
        out_shape=jax.ShapeDtypeStruct((3, T, D_MODEL), BF16),
        scratch_shapes=[slab, slab, pltpu.VMEM((SEQ, LANES), F32)],
        compiler_params=_cparams(("arbitrary", "arbitrary")),
        name="norm_perm",
    )(xf, norm_w.reshape(1, D_MODEL))


_PROJ_TM = 4096
_PROJ_TN = 1024
_PROJ_NC = 256
_PROJ_MC = 2048


def _in_proj_kernel(tbl_ref, hs_ref, w_ref, b_ref, t256_ref, t128_ref, o_ref):
    kind = tbl_ref[2, pl.program_id(1)]

    def emit(epilogue):
        def one_batch(bt, carry):
            base = pl.multiple_of(bt * SEQ, SEQ)
            for nc in range(_PROJ_TN // _PROJ_NC):
                cols = slice(nc * _PROJ_NC, (nc + 1) * _PROJ_NC)
                w = w_ref[:, cols].astype(BF16)
                for mc in range(SEQ // _PROJ_MC):
                    pos = slice(mc * _PROJ_MC, (mc + 1) * _PROJ_MC)
                    rows = pl.ds(base + mc * _PROJ_MC, _PROJ_MC)
                    z = jnp.dot(hs_ref[rows, :], w, preferred_element_type=F32) + b_ref[:, cols]
                    o_ref[rows, cols] = epilogue(z, pos).astype(BF16)
            return carry

        lax.fori_loop(0, _PROJ_TM // SEQ, one_batch, 0)

    def rope256(z, rows):
        cos, sin = t256_ref[0, rows, :], t256_ref[1, rows, :]
        parts = []
        for hh in range(_PROJ_NC // RET_HEAD_DIM):
            x1 = z[:, hh * RET_HEAD_DIM:hh * RET_HEAD_DIM + LANES]
            x2 = z[:, hh * RET_HEAD_DIM + LANES:(hh + 1) * RET_HEAD_DIM]
            parts += [x1 * cos - x2 * sin, x2 * cos + x1 * sin]
        return jnp.concatenate(parts, axis=1)

    def rope128(z, rows):
        cos_full, sin_signed = t128_ref[0, rows, :], t128_ref[1, rows, :]
        halves = []
        for hh in range(_PROJ_NC // LANES):
            x = z[:, hh * LANES:(hh + 1) * LANES]
            halves.append(x * cos_full + pltpu.roll(x, LANES // 2, 1) * sin_signed)
        return jnp.concatenate(halves, axis=1)

    epilogues = {
        _EPI_ROPE256: rope256,
        _EPI_NONE: lambda z, rows: z,
        _EPI_SILU: lambda z, rows: z * jax.nn.sigmoid(z),
        _EPI_SIGMOID: lambda z, rows: jax.nn.sigmoid(z),
        _EPI_ROPE128: rope128,
    }
    for k, fn in epilogues.items():
        pl.when(kind == k)(functools.partial(emit, fn))


def _in_proj(hs, w_in, layer, b_in, t256, t128):
    T = hs.shape[1]
    tbl = jnp.asarray(np.array(_TILE_PLAN, dtype=np.int32).T)
    assert _PROJ_TM % SEQ == 0 and SEQ % _PROJ_MC == 0
    grid_spec = pltpu.PrefetchScalarGridSpec(
        num_scalar_prefetch=1,
        grid=(T // _PROJ_TM, IN_WIDTH // _PROJ_TN),
        in_specs=[
            pl.BlockSpec((None, _PROJ_TM, D_MODEL), lambda i, j, t: (t[1, j], i, 0)),
            pl.BlockSpec((None, D_MODEL, _PROJ_TN), lambda i, j, t: (layer, 0, t[0, j])),
            pl.BlockSpec((1, _PROJ_TN), lambda i, j, t: (0, t[0, j])),
            pl.BlockSpec((None, 2, SEQ, LANES), lambda i, j, t: (t[3, j], 0, 0, 0)),
            pl.BlockSpec((None, 2, SEQ, LANES), lambda i, j, t: (t[1, j], 0, 0, 0)),
        ],
        out_specs=pl.BlockSpec((_PROJ_TM, _PROJ_TN), lambda i, j, t: (i, t[0, j])),
    )
    return pl.pallas_call(
        _in_proj_kernel,
        grid_spec=grid_spec,
        out_shape=jax.ShapeDtypeStruct((T, IN_WIDTH), BF16),
        compiler_params=_cparams(("arbitrary", "arbitrary")),
        name="in_proj",
    )(tbl, hs, w_in, b_in.reshape(1, IN_WIDTH), t256, t128)


def _rope_tables():
    pos = jnp.arange(SEQ, dtype=jnp.int32)

    def cos_sin(hd, p):
        inv = ROPE_THETA ** (-jnp.arange(0, hd, 2, dtype=F32) / hd)
        ang = p.astype(F32)[:, None] * inv[None, :]
        return jnp.cos(ang), jnp.sin(ang)

    c, s = cos_sin(RET_HEAD_DIM, pos)
    k_scale = RET_HEAD_DIM ** -0.5
    t256 = jnp.stack([jnp.stack([c, s]), jnp.stack([c * k_scale, s * k_scale])])
    layouts = []
    for d in ATT_DILATIONS:
        p = pos.reshape(SEQ // d, d).T.reshape(SEQ)
        c, s = cos_sin(ATT_HEAD_DIM, p)
        layouts.append(jnp.stack([jnp.concatenate([c, c], axis=1), jnp.concatenate([-s, s], axis=1)]))
    return t256, jnp.stack(layouts)


def _retention_kernel(q_ref, k_ref, v_ref, g_ref, nw_ref, dec_ref, qd_ref, kd_ref, cd_ref, o_ref, st_ref):
    C = RET_CHUNK
    st_ref[...] = jnp.zeros_like(st_ref)
    dec = dec_ref[...]
    qd = qd_ref[...]
    kd = kd_ref[...]
    cd = cd_ref[...]
    nw = nw_ref[...]
    for n in range(SEQ // C):
        rows = slice(n * C, (n + 1) * C)
        q = q_ref[rows, :]
        k = k_ref[rows, :]
        v = v_ref[rows, :]
        s = lax.dot_general(q, k, (((1,), (1,)), ((), ())), preferred_element_type=F32) * dec
        intra = jnp.dot(s.astype(BF16), v, preferred_element_type=F32)
        state = st_ref[...]
        inter = jnp.dot(q, state.astype(BF16), preferred_element_type=F32) * qd
        kdec = (k.astype(F32) * kd).astype(BF16)
        kv = lax.dot_general(kdec, v, (((0,), (0,)), ((), ())), preferred_element_type=F32)
        st_ref[...] = state * cd + kv
        r = intra + inter
        mu = jnp.mean(r, axis=-1, keepdims=True)
        rc = r - mu
        var = jnp.mean(rc * rc, axis=-1, keepdims=True)
        rn = rc * lax.rsqrt(var + HEAD_NORM_EPS)
        gate = g_ref[rows, :].astype(F32)
        o_ref[rows, :] = (rn * nw * (gate * jax.nn.sigmoid(gate))).astype(BF16)


def _retention_tables():
    C = RET_CHUNK
    lg = jnp.log(1.0 - 2.0 ** (-5.0 - jnp.arange(RET_HEADS, dtype=F32)))
    idx = jnp.arange(C, dtype=F32)
    diff = idx[:, None] - idx[None, :]
    intra = jnp.where(diff[None] >= 0, jnp.exp(jnp.maximum(diff, 0.0)[None] * lg[:, None, None]), 0.0)
    q_decay = jnp.exp((idx + 1.0)[None, :] * lg[:, None])
    k_decay = jnp.exp((C - 1.0 - idx)[None, :] * lg[:, None])
    chunk_decay = jnp.exp(C * lg)
    bcast = (RET_HEADS, C, RET_HEAD_DIM)
    return (intra.astype(F32),
            jnp.broadcast_to(q_decay[:, :, None], bcast).astype(F32),
            jnp.broadcast_to(k_decay[:, :, None], bcast).astype(F32),
            jnp.broadcast_to(chunk_decay[:, None, None], (RET_HEADS, 1, RET_HEAD_DIM)).astype(F32))


def _retention(z, ret_norm_w, tables):
    T = z.shape[0]
    hd = RET_HEAD_DIM
    per_head = RET_HEADS
    dec, qd, kd, cd = tables

    def zspec(col_tile):
        return pl.BlockSpec((SEQ, hd), lambda b, h: (b, col_tile * per_head + h))

    def tspec(shape):
        return pl.BlockSpec((None,) + shape, lambda b, h: (h, 0, 0))

    return pl.pallas_call(
        _retention_kernel,
        grid=(T // SEQ, RET_HEADS),
        in_specs=[zspec(_COL_RQ), zspec(_COL_RK), zspec(_COL_RV), zspec(_COL_RG),
                  pl.BlockSpec((1, hd), lambda b, h: (0, h)),
                  tspec((RET_CHUNK, RET_CHUNK)), tspec((RET_CHUNK, hd)), tspec((RET_CHUNK, hd)),
                  tspec((1, hd))],
        out_specs=pl.BlockSpec((SEQ, hd), lambda b, h: (b, h)),
        out_shape=jax.ShapeDtypeStruct((T, RET_HEADS * hd), BF16),
        scratch_shapes=[pltpu.VMEM((hd, hd), F32)],
        compiler_params=_cparams(("arbitrary", "arbitrary")),
        name="retention",
    )(z, z, z, z, ret_norm_w.reshape(1, RET_HEADS * hd), dec, qd, kd, cd)


def _attention_kernel(q0, k0, v0, q1, k1, v1, q2, k2, v2, g_ref, o_ref,
                      n2_ref, d2_ref, m2_ref, n12_ref, d12_ref, m12_ref, s_ref, p_ref, m_ref):
    blk = ATT_BLOCK
    hd = ATT_HEAD_DIM
    n_blk = SEQ // blk
    scale = hd ** -0.5
    neg_inf = jnp.float32(-jnp.inf)
    row = lax.broadcasted_iota(jnp.int32, (blk, 2 * blk), 0)
    col = lax.broadcasted_iota(jnp.int32, (blk, 2 * blk), 1)
    in_prev = col < blk
    valid = jnp.where(in_prev, -1, 1) * (row - jnp.where(in_prev, col, col - blk)) >= 0
    bias = jnp.where(valid, jnp.float32(0.0), neg_inf)
    contract_last = (((1,), (1,)), ((), ()))

    def group(d, q_ref, k_ref, v_ref, finish):
        per_seq = n_blk // d
        for bi in range(n_blk):
            q = q_ref[bi * blk:(bi + 1) * blk, :]
            if bi % per_seq == 0:
                s_ref[bi, :, :blk] = jnp.full((blk, blk), neg_inf, F32)
                s_ref[bi, :, blk:] = lax.dot_general(q, k_ref[bi * blk:(bi + 1) * blk, :], contract_last,
                                                     preferred_element_type=F32)
            else:
                s_ref[bi] = lax.dot_general(q, k_ref[(bi - 1) * blk:(bi + 1) * blk, :], contract_last,
                                            preferred_element_type=F32)
        s = s_ref[...] * scale + bias[None]
        m = jnp.max(s, axis=-1, keepdims=True)
        p_ref[...] = jnp.exp(s - m).astype(BF16)
        m_ref[...] = jnp.broadcast_to(m, (n_blk, blk, hd))
        for bi in range(n_blk):
            first = bi % per_seq == 0
            vv = v_ref[(bi if first else bi - 1) * blk:(bi + 1) * blk, :]
            v_aug = jnp.concatenate([vv, jnp.ones_like(vv)], axis=1)
            pv = jnp.dot(p_ref[bi, :, blk:] if first else p_ref[bi], v_aug, preferred_element_type=F32)
            finish(bi, pv[:, :hd], pv[:, hd:], m_ref[bi])

    d1, d4, d16 = ATT_DILATIONS
    assert (d1, d4 * d4) == (1, d16)
    rows_by4 = SEQ // d4

    def finish_d16(bi, num, den, m):
        dst = pl.ds((bi % d4) * rows_by4 + bi // d4, blk, stride=d4)
        n2_ref[dst, :] = num
        d2_ref[dst, :] = den
        m2_ref[dst, :] = m

    def finish_d4(bi, num, den, m):
        rows = slice(bi * blk, (bi + 1) * blk)
        m2 = m2_ref[rows, :]
        m12 = jnp.maximum(m, m2)
        w1, w2 = jnp.exp(m - m12), jnp.exp(m2 - m12)
        r, i = divmod(bi, rows_by4 // blk)
        dst = pl.ds(i * blk * d4 + r, blk, stride=d4)
        n12_ref[dst, :] = w1 * num + w2 * n2_ref[rows, :]
        d12_ref[dst, :] = w1 * den + w2 * d2_ref[rows, :]
        m12_ref[dst, :] = m12

    def finish_d1(bi, num, den, m):
        rows = slice(bi * blk, (bi + 1) * blk)
        m12 = m12_ref[rows, :]
        m_all = jnp.maximum(m, m12)
        w0, w12 = jnp.exp(m - m_all), jnp.exp(m12 - m_all)
        a = (w0 * num + w12 * n12_ref[rows, :]) / (w0 * den + w12 * d12_ref[rows, :])
        gate = g_ref[rows, :].astype(F32)
        o_ref[rows, :] = (a * (gate * jax.nn.sigmoid(gate))).astype(BF16)

    group(d16, q2, k2, v2, finish_d16)
    group(d4, q1, k1, v1, finish_d4)
    group(d1, q0, k0, v0, finish_d1)


def _attention(z):
    T = z.shape[0]
    hd = ATT_HEAD_DIM
    per_tile = _PROJ_TN // hd

    def zspec(col_tile):
        return pl.BlockSpec((SEQ, hd), lambda b, h: (b, col_tile * per_tile + h))

    in_specs = []
    for g in range(len(ATT_DILATIONS)):
        in_specs += [zspec(_COL_AQ + g), zspec(_COL_AK + g), zspec(_COL_AV + g)]
    in_specs.append(zspec(_COL_AG))
    return pl.pallas_call(
        _attention_kernel,
        grid=(T // SEQ, ATT_HEADS_PER_GROUP),
        in_specs=in_specs,
        out_specs=pl.BlockSpec((SEQ, hd), lambda b, h: (b, h)),
        out_shape=jax.ShapeDtypeStruct((T, ATT_HEADS_PER_GROUP * hd), BF16),
        scratch_shapes=[pltpu.VMEM((SEQ, hd), F32)] * 6 + [
                        pltpu.VMEM((SEQ // ATT_BLOCK, ATT_BLOCK, 2 * ATT_BLOCK), F32),
                        pltpu.VMEM((SEQ // ATT_BLOCK, ATT_BLOCK, 2 * ATT_BLOCK), BF16),
                        pltpu.VMEM((SEQ // ATT_BLOCK, ATT_BLOCK, hd), F32)],
        compiler_params=_cparams(("arbitrary", "arbitrary")),
        name="attention",
    )(*([z] * 10))


_CONV_ROWS = 512
_CONV_HALO = 32
_CONV_NORM_ROWS = 256


def _conv_kernel(cl_ref, cs_ref, g_ref, dw_ref, dwb_ref, lnw_ref, lnb_ref, o_ref, pad_ref, acc_ref):
    R, H = _CONV_ROWS, _CONV_HALO
    n_slab = D_MODEL // LANES

    @pl.when(pl.program_id(1) == 0)
    def _():
        pad_ref[:, 0:H, :] = jnp.zeros((n_slab, H, LANES), F32)

    lead = H - (CONV_TAPS - 1)
    for c in range(n_slab):
        cols = slice(c * LANES, (c + 1) * LANES)
        pad_ref[c, H:H + R, :] = cl_ref[:, cols].astype(F32) * cs_ref[:, cols].astype(F32)
    for c in range(n_slab):
        cols = slice(c * LANES, (c + 1) * LANES)
        acc = pad_ref[c, lead:lead + R, :] * dw_ref[0:1, cols]
        for k in range(1, CONV_TAPS):
            acc = acc + pad_ref[c, lead + k:lead + k + R, :] * dw_ref[k:k + 1, cols]
        acc_ref[:, cols] = acc
    for c in range(n_slab):
        pad_ref[c, 0:H, :] = pad_ref[c, R:R + H, :]

    def norm_rows(i, carry):
        rows = pl.ds(pl.multiple_of(i * _CONV_NORM_ROWS, _CONV_NORM_ROWS), _CONV_NORM_ROWS)
        c = acc_ref[rows, :] + dwb_ref[...]
        mu = jnp.mean(c, axis=-1, keepdims=True)
        cc = c - mu
        var = jnp.mean(cc * cc, axis=-1, keepdims=True)
        y = cc * lax.rsqrt(var + HEAD_NORM_EPS) * lnw_ref[...] + lnb_ref[...]
        y = y * jax.nn.sigmoid(y)
        o_ref[rows, :] = (y * g_ref[rows, :].astype(F32)).astype(BF16)
        return carry

    lax.fori_loop(0, R // _CONV_NORM_ROWS, norm_rows, 0)


def _conv(z, dw_w, dw_b, ln_w, ln_b):
    T = z.shape[0]
    R = _CONV_ROWS
    n_r = SEQ // R

    def zspec(col_tile):
        return pl.BlockSpec((R, D_MODEL), lambda b, j: (b * n_r + j, col_tile))

    def vec():
        return pl.BlockSpec((1, D_MODEL), lambda b, j: (0, 0))

    return pl.pallas_call(
        _conv_kernel,
        grid=(T // SEQ, n_r),
        in_specs=[zspec(_COL_CLIN), zspec(_COL_CGATE), zspec(_COL_CG),
                  pl.BlockSpec((CONV_TAPS, D_MODEL), lambda b, j: (0, 0)), vec(), vec(), vec()],
        out_specs=pl.BlockSpec((R, D_MODEL), lambda b, j: (b * n_r + j, 0)),
        out_shape=jax.ShapeDtypeStruct((T, D_MODEL), BF16),
        scratch_shapes=[pltpu.VMEM((D_MODEL // LANES, R + _CONV_HALO, LANES), F32),
                        pltpu.VMEM((R, D_MODEL), F32)],
        compiler_params=_cparams(("arbitrary", "arbitrary")),
        name="conv",
    )(z, z, z, dw_w, dw_b.reshape(1, D_MODEL), ln_w.reshape(1, D_MODEL), ln_b.reshape(1, D_MODEL))


_OUT_TM = 512


def _out_proj_kernel(final, r_ref, a_ref, c_ref, g0_ref, g1_ref, g2_ref, x_ref,
                     wr_ref, wa_ref, wc_ref, wo_ref, fw_ref, o_ref):
    def branch(in_ref, w_ref, gate_ref):
        y = jnp.dot(in_ref[...], w_ref[...], preferred_element_type=F32)
        return jax.nn.sigmoid(gate_ref[...].astype(F32)) * y

    merged = branch(r_ref, wr_ref, g0_ref) + branch(a_ref, wa_ref, g1_ref) + branch(c_ref, wc_ref, g2_ref)
    out = x_ref[...] + jnp.dot(merged.astype(BF16), wo_ref[...], preferred_element_type=F32)
    if final:
        out = out * lax.rsqrt(jnp.mean(out * out, axis=-1, keepdims=True) + NORM_EPS) * fw_ref[...]
    o_ref[...] = out


def _out_proj(rg, ag, cg, z, xf, w_r, w_a, w_c, w_o, final_w, final):
    T = xf.shape[0]
    tm = _OUT_TM

    def rows(col_tile=0):
        return pl.BlockSpec((tm, D_MODEL), lambda i: (i, col_tile))

    def weight():
        return pl.BlockSpec((D_MODEL, D_MODEL), lambda i: (0, 0))

    return pl.pallas_call(
        functools.partial(_out_proj_kernel, final),
        grid=(T // tm,),
        in_specs=[rows(), rows(), rows(), rows(_COL_MG), rows(_COL_MG + 1), rows(_COL_MG + 2), rows(),
                  weight(), weight(), weight(), weight(),
                  pl.BlockSpec((1, D_MODEL), lambda i: (0, 0))],
        out_specs=rows(),
        out_shape=jax.ShapeDtypeStruct((T, D_MODEL), F32),
        compiler_params=_cparams(("arbitrary",)),
        name="out_proj_final" if final else "out_proj",
    )(rg, ag, cg, z, z, z, xf, w_r, w_a, w_c, w_o, final_w.reshape(1, D_MODEL))


def kernel(x, norm_w, w_in, b_in, ret_norm_w, ret_w_o, att_w_o, conv_dw_w, conv_dw_b,
           conv_norm_w, conv_norm_b, conv_w_o, w_out, final_norm_w):
    B, S, D = x.shape
    assert (S, D) == (SEQ, D_MODEL) and w_in.shape[-1] == IN_WIDTH
    depth = w_in.shape[0]
    t256, t128 = _rope_tables()
    ret_tables = _retention_tables()
    xf = x.reshape(B * S, D)
    for l in range(depth):
        hs = _norm_perm(xf, norm_w[l])
        z = _in_proj(hs, w_in, l, b_in[l], t256, t128)
        rg = _retention(z, ret_norm_w[l], ret_tables)
        ag = _attention(z)
        cg = _conv(z, conv_dw_w[l], conv_dw_b[l], conv_norm_w[l], conv_norm_b[l])
        xf = _out_proj(rg, ag, cg, z, xf,
                       ret_w_o[l].astype(BF16), att_w_o[l].astype(BF16), conv_w_o[l].astype(BF16),
                       w_out[l].astype(BF16), final_norm_w, final=(l == depth - 1))
    return xf.reshape(B, S, D)
```

```python
import functools

import numpy as np
import jax
import jax.numpy as jnp
from jax import lax
from jax.experimental import pallas as pl
from jax.experimental.pallas import tpu as pltpu

F32 = jnp.float32
BF16 = jnp.bfloat16

D_MODEL = 1024
SEQ = 2048
ROPE_THETA = 10000.0
NORM_EPS = 1e-6
HEAD_NORM_EPS = 1e-5
RET_HEADS = 4
RET_HEAD_DIM = 256
RET_CHUNK = 256
ATT_DILATIONS = (1, 4, 16)
ATT_HEADS_PER_GROUP = 8
ATT_HEAD_DIM = 128
ATT_BLOCK = 128
CONV_TAPS = 31
IN_WIDTH = 20480

LANES = 128
SUBLANES = 8
VMEM_LIMIT = 56 * 1024 * 1024

_COL_RQ, _COL_RK, _COL_RV, _COL_RG = 0, 1, 2, 3
_COL_AQ, _COL_AK, _COL_AV, _COL_AG = 4, 7, 10, 13
_COL_CLIN, _COL_CGATE, _COL_CG, _COL_MG = 14, 15, 16, 17

_EPI_ROPE256, _EPI_NONE, _EPI_SILU, _EPI_SIGMOID, _EPI_ROPE128 = 0, 1, 2, 3, 4

_TILE_PLAN = (
    (_COL_RQ, 0, _EPI_ROPE256, 0), (_COL_RK, 0, _EPI_ROPE256, 1),
    (_COL_RV, 0, _EPI_NONE, 0), (_COL_RG, 0, _EPI_NONE, 0),
    (_COL_AQ, 0, _EPI_ROPE128, 0), (_COL_AK, 0, _EPI_ROPE128, 0), (_COL_AV, 0, _EPI_NONE, 0),
    (_COL_AG, 0, _EPI_NONE, 0), (_COL_CLIN, 0, _EPI_NONE, 0), (_COL_CGATE, 0, _EPI_SIGMOID, 0),
    (_COL_CG, 0, _EPI_SILU, 0),
    (_COL_MG, 0, _EPI_NONE, 0), (_COL_MG + 1, 0, _EPI_NONE, 0), (_COL_MG + 2, 0, _EPI_NONE, 0),
    (_COL_AQ + 1, 1, _EPI_ROPE128, 0), (_COL_AK + 1, 1, _EPI_ROPE128, 0), (_COL_AV + 1, 1, _EPI_NONE, 0),
    (_COL_AQ + 2, 2, _EPI_ROPE128, 0), (_COL_AK + 2, 2, _EPI_ROPE128, 0), (_COL_AV + 2, 2, _EPI_NONE, 0),
)


def _cparams(sem):
    return pltpu.CompilerParams(dimension_semantics=sem, vmem_limit_bytes=VMEM_LIMIT)


_NORM_SLAB_BUFS = 2


def _norm_perm_kernel(x_ref, w_ref, hs_ref, slab_ref, slab4_ref):
    d4, d16 = ATT_DILATIONS[1:]
    assert d4 * d4 == d16
    L4, L16 = SEQ // d4, SEQ // d16
    x = x_ref[...]
    inv = lax.rsqrt(jnp.mean(x * x, axis=-1, keepdims=True) + NORM_EPS)
    for c in range(D_MODEL // LANES):
        cols = slice(c * LANES, (c + 1) * LANES)
        buf = c % _NORM_SLAB_BUFS
        h = x_ref[:, cols] * inv * w_ref[:, cols]
        slab_ref[buf] = h
        hs_ref[0, :, cols] = h.astype(BF16)
        for r in range(d4):
            v = slab_ref[buf, pl.ds(r, L4, stride=d4), :]
            slab4_ref[buf, r * L4:(r + 1) * L4, :] = v
            hs_ref[1, r * L4:(r + 1) * L4, cols] = v.astype(BF16)
        for r in range(d16):
            v = slab4_ref[buf, pl.ds((r % d4) * L4 + r // d4, L16, stride=d4), :]
            hs_ref[2, r * L16:(r + 1) * L16, cols] = v.astype(BF16)


def _norm_perm(xf, norm_w):
    T = xf.shape[0]
    slab = pltpu.VMEM((_NORM_SLAB_BUFS, SEQ, LANES), F32)
    return pl.pallas_call(
        _norm_perm_kernel,
        grid=(T // SEQ,),
        in_specs=[pl.BlockSpec((SEQ, D_MODEL), lambda b: (b, 0)),
                  pl.BlockSpec((1, D_MODEL), lambda b: (0, 0))],
        out_specs=pl.BlockSpec((3, SEQ, D_MODEL), lambda b: (0, b, 0)),
        out_shape=jax.ShapeDtypeStruct((3, T, D_MODEL), BF16),
        scratch_shapes=[slab, slab],
        compiler_params=_cparams(("arbitrary",)),
        name="norm_perm",
    )(xf, norm_w.reshape(1, D_MODEL))


_PROJ_TM = 4096
_PROJ_TN = 1024
_PROJ_NC = 256
_PROJ_MC = 2048


def _in_proj_kernel(tbl_ref, hs_ref, w_ref, b_ref, t256_ref, t128_ref, o_ref):
    kind = tbl_ref[2, pl.program_id(1)]

    def emit(epilogue):
        def one_batch(bt, carry):
            base = pl.multiple_of(bt * SEQ, SEQ)
            for nc in range(_PROJ_TN // _PROJ_NC):
                cols = slice(nc * _PROJ_NC, (nc + 1) * _PROJ_NC)
                w = w_ref[:, cols].astype(BF16)
                for mc in range(SEQ // _PROJ_MC):
                    pos = slice(mc * _PROJ_MC, (mc + 1) * _PROJ_MC)
                    rows = pl.ds(base + mc * _PROJ_MC, _PROJ_MC)
                    z = jnp.dot(hs_ref[rows, :], w, preferred_element_type=F32) + b_ref[:, cols]
                    o_ref[rows, cols] = epilogue(z, pos).astype(BF16)
            return carry

        lax.fori_loop(0, _PROJ_TM // SEQ, one_batch, 0)

    def rope256(z, rows):
        cos, sin = t256_ref[0, rows, :], t256_ref[1, rows, :]
        parts = []
        for hh in range(_PROJ_NC // RET_HEAD_DIM):
            x1 = z[:, hh * RET_HEAD_DIM:hh * RET_HEAD_DIM + LANES]
            x2 = z[:, hh * RET_HEAD_DIM + LANES:(hh + 1) * RET_HEAD_DIM]
            parts += [x1 * cos - x2 * sin, x2 * cos + x1 * sin]
        return jnp.concatenate(parts, axis=1)

    def rope128(z, rows):
        cos_full, sin_signed = t128_ref[0, rows, :], t128_ref[1, rows, :]
        halves = []
        for hh in range(_PROJ_NC // LANES):
            x = z[:, hh * LANES:(hh + 1) * LANES]
            halves.append(x * cos_full + pltpu.roll(x, LANES // 2, 1) * sin_signed)
        return jnp.concatenate(halves, axis=1)

    epilogues = {
        _EPI_ROPE256: rope256,
        _EPI_NONE: lambda z, rows: z,
        _EPI_SILU: lambda z, rows: z * jax.nn.sigmoid(z),
        _EPI_SIGMOID: lambda z, rows: jax.nn.sigmoid(z),
        _EPI_ROPE128: rope128,
    }
    for k, fn in epilogues.items():
        pl.when(kind == k)(functools.partial(emit, fn))


def _in_proj(hs, w_in, layer, b_in, t256, t128):
    T = hs.shape[1]
    tbl = jnp.asarray(np.array(_TILE_PLAN, dtype=np.int32).T)
    assert _PROJ_TM % SEQ == 0 and SEQ % _PROJ_MC == 0
    grid_spec = pltpu.PrefetchScalarGridSpec(
        num_scalar_prefetch=1,
        grid=(T // _PROJ_TM, IN_WIDTH // _PROJ_TN),
        in_specs=[
            pl.BlockSpec((None, _PROJ_TM, D_MODEL), lambda i, j, t: (t[1, j], i, 0)),
            pl.BlockSpec((None, D_MODEL, _PROJ_TN), lambda i, j, t: (layer, 0, t[0, j])),
            pl.BlockSpec((1, _PROJ_TN), lambda i, j, t: (0, t[0, j])),
            pl.BlockSpec((None, 2, SEQ, LANES), lambda i, j, t: (t[3, j], 0, 0, 0)),
            pl.BlockSpec((None, 2, SEQ, LANES), lambda i, j, t: (t[1, j], 0, 0, 0)),
        ],
        out_specs=pl.BlockSpec((_PROJ_TM, _PROJ_TN), lambda i, j, t: (i, t[0, j])),
    )
    return pl.pallas_call(
        _in_proj_kernel,
        grid_spec=grid_spec,
        out_shape=jax.ShapeDtypeStruct((T, IN_WIDTH), BF16),
        compiler_params=_cparams(("arbitrary", "arbitrary")),
        name="in_proj",
    )(tbl, hs, w_in, b_in.reshape(1, IN_WIDTH), t256, t128)


def _rope_tables():
    pos = jnp.arange(SEQ, dtype=jnp.int32)

    def cos_sin(hd, p):
        inv = ROPE_THETA ** (-jnp.arange(0, hd, 2, dtype=F32) / hd)
        ang = p.astype(F32)[:, None] * inv[None, :]
        return jnp.cos(ang), jnp.sin(ang)

    c, s = cos_sin(RET_HEAD_DIM, pos)
    k_scale = RET_HEAD_DIM ** -0.5
    t256 = jnp.stack([jnp.stack([c, s]), jnp.stack([c * k_scale, s * k_scale])])
    layouts = []
    for d in ATT_DILATIONS:
        p = pos.reshape(SEQ // d, d).T.reshape(SEQ)
        c, s = cos_sin(ATT_HEAD_DIM, p)
        layouts.append(jnp.stack([jnp.concatenate([c, c], axis=1), jnp.concatenate([-s, s], axis=1)]))
    return t256, jnp.stack(layouts)


def _retention_kernel(q_ref, k_ref, v_ref, g_ref, nw_ref, dec_ref, qd_ref, kd_ref, cd_ref, o_ref, st_ref):
    C = RET_CHUNK
    st_ref[...] = jnp.zeros_like(st_ref)
    dec = dec_ref[...]
    qd = qd_ref[...]
    kd = kd_ref[...]
    cd = cd_ref[...]
    nw = nw_ref[...]
    for n in range(SEQ // C):
        rows = slice(n * C, (n + 1) * C)
        q = q_ref[rows, :]
        k = k_ref[rows, :]
        v = v_ref[rows, :]
        s = lax.dot_general(q, k, (((1,), (1,)), ((), ())), preferred_element_type=F32) * dec
        intra = jnp.dot(s.astype(BF16), v, preferred_element_type=F32)
        state = st_ref[...]
        inter = jnp.dot(q, state.astype(BF16), preferred_element_type=F32) * qd
        kdec = (k.astype(F32) * kd).astype(BF16)
        kv = lax.dot_general(kdec, v, (((0,), (0,)), ((), ())), preferred_element_type=F32)
        st_ref[...] = state * cd + kv
        r = intra + inter
        mu = jnp.mean(r, axis=-1, keepdims=True)
        rc = r - mu
        var = jnp.mean(rc * rc, axis=-1, keepdims=True)
        rn = rc * lax.rsqrt(var + HEAD_NORM_EPS)
        gate = g_ref[rows, :].astype(F32)
        o_ref[rows, :] = (rn * nw * (gate * jax.nn.sigmoid(gate))).astype(BF16)


def _retention_tables():
    C = RET_CHUNK
    lg = jnp.log(1.0 - 2.0 ** (-5.0 - jnp.arange(RET_HEADS, dtype=F32)))
    idx = jnp.arange(C, dtype=F32)
    diff = idx[:, None] - idx[None, :]
    intra = jnp.where(diff[None] >= 0, jnp.exp(jnp.maximum(diff, 0.0)[None] * lg[:, None, None]), 0.0)
    q_decay = jnp.exp((idx + 1.0)[None, :] * lg[:, None])
    k_decay = jnp.exp((C - 1.0 - idx)[None, :] * lg[:, None])
    chunk_decay = jnp.exp(C * lg)
    bcast = (RET_HEADS, C, RET_HEAD_DIM)
    return (intra.astype(F32),
            jnp.broadcast_to(q_decay[:, :, None], bcast).astype(F32),
            jnp.broadcast_to(k_decay[:, :, None], bcast).astype(F32),
            jnp.broadcast_to(chunk_decay[:, None, None], (RET_HEADS, 1, RET_HEAD_DIM)).astype(F32))


def _retention(z, ret_norm_w, tables):
    T = z.shape[0]
    hd = RET_HEAD_DIM
    per_head = RET_HEADS
    dec, qd, kd, cd = tables

    def zspec(col_tile):
        return pl.BlockSpec((SEQ, hd), lambda b, h: (b, col_tile * per_head + h))

    def tspec(shape):
        return pl.BlockSpec((None,) + shape, lambda b, h: (h, 0, 0))

    return pl.pallas_call(
        _retention_kernel,
        grid=(T // SEQ, RET_HEADS),
        in_specs=[zspec(_COL_RQ), zspec(_COL_RK), zspec(_COL_RV), zspec(_COL_RG),
                  pl.BlockSpec((1, hd), lambda b, h: (0, h)),
                  tspec((RET_CHUNK, RET_CHUNK)), tspec((RET_CHUNK, hd)), tspec((RET_CHUNK, hd)),
                  tspec((1, hd))],
        out_specs=pl.BlockSpec((SEQ, hd), lambda b, h: (b, h)),
        out_shape=jax.ShapeDtypeStruct((T, RET_HEADS * hd), BF16),
        scratch_shapes=[pltpu.VMEM((hd, hd), F32)],
        compiler_params=_cparams(("arbitrary", "arbitrary")),
        name="retention",
    )(z, z, z, z, ret_norm_w.reshape(1, RET_HEADS * hd), dec, qd, kd, cd)


def _attention_kernel(q0, k0, v0, q1, k1, v1, q2, k2, v2, g_ref, o_ref,
                      n2_ref, d2_ref, m2_ref, n12_ref, d12_ref, m12_ref, s_ref, p_ref, m_ref):
    blk = ATT_BLOCK
    hd = ATT_HEAD_DIM
    n_blk = SEQ // blk
    scale = hd ** -0.5
    neg_inf = jnp.float32(-jnp.inf)
    row = lax.broadcasted_iota(jnp.int32, (blk, 2 * blk), 0)
    col = lax.broadcasted_iota(jnp.int32, (blk, 2 * blk), 1)
    in_prev = col < blk
    valid = jnp.where(in_prev, -1, 1) * (row - jnp.where(in_prev, col, col - blk)) >= 0
    bias = jnp.where(valid, jnp.float32(0.0), neg_inf)
    contract_last = (((1,), (1,)), ((), ()))

    def group(d, q_ref, k_ref, v_ref, finish):
        per_seq = n_blk // d
        for bi in range(n_blk):
            q = q_ref[bi * blk:(bi + 1) * blk, :]
            if bi % per_seq == 0:
                s_ref[bi, :, :blk] = jnp.full((blk, blk), neg_inf, F32)
                s_ref[bi, :, blk:] = lax.dot_general(q, k_ref[bi * blk:(bi + 1) * blk, :], contract_last,
                                                     preferred_element_type=F32)
            else:
                s_ref[bi] = lax.dot_general(q, k_ref[(bi - 1) * blk:(bi + 1) * blk, :], contract_last,
                                            preferred_element_type=F32)
        s = s_ref[...] * scale + bias[None]
        m = jnp.max(s, axis=-1, keepdims=True)
        p_ref[...] = jnp.exp(s - m).astype(BF16)
        m_ref[...] = jnp.broadcast_to(m, (n_blk, blk, hd))
        for bi in range(n_blk):
            first = bi % per_seq == 0
            vv = v_ref[(bi if first else bi - 1) * blk:(bi + 1) * blk, :]
            v_aug = jnp.concatenate([vv, jnp.ones_like(vv)], axis=1)
            pv = jnp.dot(p_ref[bi, :, blk:] if first else p_ref[bi], v_aug, preferred_element_type=F32)
            finish(bi, pv[:, :hd], pv[:, hd:], m_ref[bi])

    d1, d4, d16 = ATT_DILATIONS
    assert (d1, d4 * d4) == (1, d16)
    rows_by4 = SEQ // d4

    def finish_d16(bi, num, den, m):
        dst = pl.ds((bi % d4) * rows_by4 + bi // d4, blk, stride=d4)
        n2_ref[dst, :] = num
        d2_ref[dst, :] = den
        m2_ref[dst, :] = m

    def finish_d4(bi, num, den, m):
        rows = slice(bi * blk, (bi + 1) * blk)
        m2 = m2_ref[rows, :]
        m12 = jnp.maximum(m, m2)
        w1, w2 = jnp.exp(m - m12), jnp.exp(m2 - m12)
        r, i = divmod(bi, rows_by4 // blk)
        dst = pl.ds(i * blk * d4 + r, blk, stride=d4)
        n12_ref[dst, :] = w1 * num + w2 * n2_ref[rows, :]
        d12_ref[dst, :] = w1 * den + w2 * d2_ref[rows, :]
        m12_ref[dst, :] = m12

    def finish_d1(bi, num, den, m):
        rows = slice(bi * blk, (bi + 1) * blk)
        m12 = m12_ref[rows, :]
        m_all = jnp.maximum(m, m12)
        w0, w12 = jnp.exp(m - m_all), jnp.exp(m12 - m_all)
        a = (w0 * num + w12 * n12_ref[rows, :]) / (w0 * den + w12 * d12_ref[rows, :])
        gate = g_ref[rows, :].astype(F32)
        o_ref[rows, :] = (a * (gate * jax.nn.sigmoid(gate))).astype(BF16)

    group(d16, q2, k2, v2, finish_d16)
    group(d4, q1, k1, v1, finish_d4)
    group(d1, q0, k0, v0, finish_d1)


def _attention(z):
    T = z.shape[0]
    hd = ATT_HEAD_DIM
    per_tile = _PROJ_TN // hd

    def zspec(col_tile):
        return pl.BlockSpec((SEQ, hd), lambda b, h: (b, col_tile * per_tile + h))

    in_specs = []
    for g in range(len(ATT_DILATIONS)):
        in_specs += [zspec(_COL_AQ + g), zspec(_COL_AK + g), zspec(_COL_AV + g)]
    in_specs.append(zspec(_COL_AG))
    return pl.pallas_call(
        _attention_kernel,
        grid=(T // SEQ, ATT_HEADS_PER_GROUP),
        in_specs=in_specs,
        out_specs=pl.BlockSpec((SEQ, hd), lambda b, h: (b, h)),
        out_shape=jax.ShapeDtypeStruct((T, ATT_HEADS_PER_GROUP * hd), BF16),
        scratch_shapes=[pltpu.VMEM((SEQ, hd), F32)] * 6 + [
                        pltpu.VMEM((SEQ // ATT_BLOCK, ATT_BLOCK, 2 * ATT_BLOCK), F32),
                        pltpu.VMEM((SEQ // ATT_BLOCK, ATT_BLOCK, 2 * ATT_BLOCK), BF16),
                        pltpu.VMEM((SEQ // ATT_BLOCK, ATT_BLOCK, hd), F32)],
        compiler_params=_cparams(("arbitrary", "arbitrary")),
        name="attention",
    )(*([z] * 10))


_CONV_ROWS = 512
_CONV_HALO = 32
_CONV_NORM_ROWS = 256


def _conv_kernel(cl_ref, cs_ref, g_ref, dw_ref, dwb_ref, lnw_ref, lnb_ref, o_ref, pad_ref, acc_ref):
    R, H = _CONV_ROWS, _CONV_HALO
    n_slab = D_MODEL // LANES

    @pl.when(pl.program_id(1) == 0)
    def _():
        pad_ref[:, 0:H, :] = jnp.zeros((n_slab, H, LANES), F32)

    lead = H - (CONV_TAPS - 1)
    for c in range(n_slab):
        cols = slice(c * LANES, (c + 1) * LANES)
        pad_ref[c, H:H + R, :] = cl_ref[:, cols].astype(F32) * cs_ref[:, cols].astype(F32)
    for c in range(n_slab):
        cols = slice(c * LANES, (c + 1) * LANES)
        acc = pad_ref[c, lead:lead + R, :] * dw_ref[0:1, cols]
        for k in range(1, CONV_TAPS):
            acc = acc + pad_ref[c, lead + k:lead + k + R, :] * dw_ref[k:k + 1, cols]
        acc_ref[:, cols] = acc
    for c in range(n_slab):
        pad_ref[c, 0:H, :] = pad_ref[c, R:R + H, :]

    def norm_rows(i, carry):
        rows = pl.ds(pl.multiple_of(i * _CONV_NORM_ROWS, _CONV_NORM_ROWS), _CONV_NORM_ROWS)
        c = acc_ref[rows, :] + dwb_ref[...]
        mu = jnp.mean(c, axis=-1, keepdims=True)
        cc = c - mu
        var = jnp.mean(cc * cc, axis=-1, keepdims=True)
        y = cc * lax.rsqrt(var + HEAD_NORM_EPS) * lnw_ref[...] + lnb_ref[...]
        y = y * jax.nn.sigmoid(y)
        o_ref[rows, :] = (y * g_ref[rows, :].astype(F32)).astype(BF16)
        return carry

    lax.fori_loop(0, R // _CONV_NORM_ROWS, norm_rows, 0)


def _conv(z, dw_w, dw_b, ln_w, ln_b):
    T = z.shape[0]
    R = _CONV_ROWS
    n_r = SEQ // R

    def zspec(col_tile):
        return pl.BlockSpec((R, D_MODEL), lambda b, j: (b * n_r + j, col_tile))

    def vec():
        return pl.BlockSpec((1, D_MODEL), lambda b, j: (0, 0))

    return pl.pallas_call(
        _conv_kernel,
        grid=(T // SEQ, n_r),
        in_specs=[zspec(_COL_CLIN), zspec(_COL_CGATE), zspec(_COL_CG),
                  pl.BlockSpec((CONV_TAPS, D_MODEL), lambda b, j: (0, 0)), vec(), vec(), vec()],
        out_specs=pl.BlockSpec((R, D_MODEL), lambda b, j: (b * n_r + j, 0)),
        out_shape=jax.ShapeDtypeStruct((T, D_MODEL), BF16),
        scratch_shapes=[pltpu.VMEM((D_MODEL // LANES, R + _CONV_HALO, LANES), F32),
                        pltpu.VMEM((R, D_MODEL), F32)],
        compiler_params=_cparams(("arbitrary", "arbitrary")),
        name="conv",
    )(z, z, z, dw_w, dw_b.reshape(1, D_MODEL), ln_w.reshape(1, D_MODEL), ln_b.reshape(1, D_MODEL))


_OUT_TM = 512


def _out_proj_kernel(final, r_ref, a_ref, c_ref, g0_ref, g1_ref, g2_ref, x_ref,
                     wr_ref, wa_ref, wc_ref, wo_ref, fw_ref, o_ref):
    def branch(in_ref, w_ref, gate_ref):
        y = jnp.dot(in_ref[...], w_ref[...], preferred_element_type=F32)
        return jax.nn.sigmoid(gate_ref[...].astype(F32)) * y

    merged = branch(r_ref, wr_ref, g0_ref) + branch(a_ref, wa_ref, g1_ref) + branch(c_ref, wc_ref, g2_ref)
    out = x_ref[...] + jnp.dot(merged.astype(BF16), wo_ref[...], preferred_element_type=F32)
    if final:
        out = out * lax.rsqrt(jnp.mean(out * out, axis=-1, keepdims=True) + NORM_EPS) * fw_ref[...]
    o_ref[...] = out


def _out_proj(rg, ag, cg, z, xf, w_r, w_a, w_c, w_o, final_w, final):
    T = xf.shape[0]
    tm = _OUT_TM

    def rows(col_tile=0):
        return pl.BlockSpec((tm, D_MODEL), lambda i: (i, col_tile))

    def weight():
        return pl.BlockSpec((D_MODEL, D_MODEL), lambda i: (0, 0))

    return pl.pallas_call(
        functools.partial(_out_proj_kernel, final),
        grid=(T // tm,),
        in_specs=[rows(), rows(), rows(), rows(_COL_MG), rows(_COL_MG + 1), rows(_COL_MG + 2), rows(),
                  weight(), weight(), weight(), weight(),
                  pl.BlockSpec((1, D_MODEL), lambda i: (0, 0))],
        out_specs=rows(),
        out_shape=jax.ShapeDtypeStruct((T, D_MODEL), F32),
        compiler_params=_cparams(("arbitrary",)),
        name="out_proj_final" if final else "out_proj",
    )(rg, ag, cg, z, z, z, xf, w_r, w_a, w_c, w_o, final_w.reshape(1, D_MODEL))


def kernel(x, norm_w, w_in, b_in, ret_norm_w, ret_w_o, att_w_o, conv_dw_w, conv_dw_b,
           conv_norm_w, conv_norm_b, conv_w_o, w_out, final_norm_w):
    B, S, D = x.shape
    assert (S, D) == (SEQ, D_MODEL) and w_in.shape[-1] == IN_WIDTH
    depth = w_in.shape[0]
    t256, t128 = _rope_tables()
    ret_tables = _retention_tables()
    xf = x.reshape(B * S, D)
    for l in range(depth):
        hs = _norm_perm(xf, norm_w[l])
        z = _in_proj(hs, w_in, l, b_in[l], t256, t128)
        rg = _retention(z, ret_norm_w[l], ret_tables)
        ag = _attention(z)
        cg = _conv(z, conv_dw_w[l], conv_dw_b[l], conv_norm_w[l], conv_norm_b[l])
        xf = _out_proj(rg, ag, cg, z, xf,
                       ret_w_o[l].astype(BF16), att_w_o[l].astype(BF16), conv_w_o[l].astype(BF16),
                       w_out[l].astype(BF16), final_norm_w, final=(l == depth - 1))
    return xf.reshape(B, S, D)
```

```python
import functools

import numpy as np
import jax
import jax.numpy as jnp
from jax import lax
from jax.experimental import pallas as pl
from jax.experimental.pallas import tpu as pltpu

F32 = jnp.float32
BF16 = jnp.bfloat16

D_MODEL = 1024
SEQ = 2048
ROPE_THETA = 10000.0
NORM_EPS = 1e-6
HEAD_NORM_EPS = 1e-5
RET_HEADS = 4
RET_HEAD_DIM = 256
RET_CHUNK = 256
ATT_DILATIONS = (1, 4, 16)
ATT_HEADS_PER_GROUP = 8
ATT_HEAD_DIM = 128
ATT_BLOCK = 128
CONV_TAPS = 31
IN_WIDTH = 20480

LANES = 128
SUBLANES = 8
VMEM_LIMIT = 56 * 1024 * 1024

_COL_RQ, _COL_RK, _COL_RV, _COL_RG = 0, 1, 2, 3
_COL_AQ, _COL_AK, _COL_AV, _COL_AG = 4, 7, 10, 13
_COL_CLIN, _COL_CGATE, _COL_CG, _COL_MG = 14, 15, 16, 17

_EPI_ROPE256, _EPI_NONE, _EPI_SILU, _EPI_SIGMOID, _EPI_ROPE128 = 0, 1, 2, 3, 4

_TILE_PLAN = (
    (_COL_RQ, 0, _EPI_ROPE256, 0), (_COL_RK, 0, _EPI_ROPE256, 1),
    (_COL_RV, 0, _EPI_NONE, 0), (_COL_RG, 0, _EPI_NONE, 0),
    (_COL_AQ, 0, _EPI_ROPE128, 0), (_COL_AK, 0, _EPI_ROPE128, 0), (_COL_AV, 0, _EPI_NONE, 0),
    (_COL_AG, 0, _EPI_NONE, 0), (_COL_CLIN, 0, _EPI_NONE, 0), (_COL_CGATE, 0, _EPI_SIGMOID, 0),
    (_COL_CG, 0, _EPI_SILU, 0),
    (_COL_MG, 0, _EPI_NONE, 0), (_COL_MG + 1, 0, _EPI_NONE, 0), (_COL_MG + 2, 0, _EPI_NONE, 0),
    (_COL_AQ + 1, 1, _EPI_ROPE128, 0), (_COL_AK + 1, 1, _EPI_ROPE128, 0), (_COL_AV + 1, 1, _EPI_NONE, 0),
    (_COL_AQ + 2, 2, _EPI_ROPE128, 0), (_COL_AK + 2, 2, _EPI_ROPE128, 0), (_COL_AV + 2, 2, _EPI_NONE, 0),
)


def _cparams(sem):
    return pltpu.CompilerParams(dimension_semantics=sem, vmem_limit_bytes=VMEM_LIMIT)


_NORM_SLAB_BUFS = 2


def _norm_perm_kernel(x_ref, w_ref, hs_ref, slab_ref, slab4_ref):
    d4, d16 = ATT_DILATIONS[1:]
    assert d4 * d4 == d16
    L4, L16 = SEQ // d4, SEQ // d16
    x = x_ref[...]
    inv = lax.rsqrt(jnp.mean(x * x, axis=-1, keepdims=True) + NORM_EPS)
    for c in range(D_MODEL // LANES):
        cols = slice(c * LANES, (c + 1) * LANES)
        buf = c % _NORM_SLAB_BUFS
        h = x_ref[:, cols] * inv * w_ref[:, cols]
        slab_ref[buf] = h
        hs_ref[0, :, cols] = h.astype(BF16)
        for r in range(d4):
            v = slab_ref[buf, pl.ds(r, L4, stride=d4), :]
            slab4_ref[buf, r * L4:(r + 1) * L4, :] = v
            hs_ref[1, r * L4:(r + 1) * L4, cols] = v.astype(BF16)
        for r in range(d16):
            v = slab4_ref[buf, pl.ds((r % d4) * L4 + r // d4, L16, stride=d4), :]
            hs_ref[2, r * L16:(r + 1) * L16, cols] = v.astype(BF16)


def _norm_perm(xf, norm_w):
    T = xf.shape[0]
    slab = pltpu.VMEM((_NORM_SLAB_BUFS, SEQ, LANES), F32)
    return pl.pallas_call(
        _norm_perm_kernel,
        grid=(T // SEQ,),
        in_specs=[pl.BlockSpec((SEQ, D_MODEL), lambda b: (b, 0)),
                  pl.BlockSpec((1, D_MODEL), lambda b: (0, 0))],
        out_specs=pl.BlockSpec((3, SEQ, D_MODEL), lambda b: (0, b, 0)),
        out_shape=jax.ShapeDtypeStruct((3, T, D_MODEL), BF16),
        scratch_shapes=[slab, slab],
        compiler_params=_cparams(("arbitrary",)),
        name="norm_perm",
    )(xf, norm_w.reshape(1, D_MODEL))


_PROJ_TM = 4096
_PROJ_TN = 1024
_PROJ_NC = 256
_PROJ_MC = 2048


def _in_proj_kernel(tbl_ref, hs_ref, w_ref, b_ref, t256_ref, t128_ref, o_ref):
    kind = tbl_ref[2, pl.program_id(1)]

    def emit(epilogue):
        def one_batch(bt, carry):
            base = pl.multiple_of(bt * SEQ, SEQ)
            for nc in range(_PROJ_TN // _PROJ_NC):
                cols = slice(nc * _PROJ_NC, (nc + 1) * _PROJ_NC)
                w = w_ref[:, cols].astype(BF16)
                for mc in range(SEQ // _PROJ_MC):
                    pos = slice(mc * _PROJ_MC, (mc + 1) * _PROJ_MC)
                    rows = pl.ds(base + mc * _PROJ_MC, _PROJ_MC)
                    z = jnp.dot(hs_ref[rows, :], w, preferred_element_type=F32) + b_ref[:, cols]
                    o_ref[rows, cols] = epilogue(z, pos).astype(BF16)
            return carry

        lax.fori_loop(0, _PROJ_TM // SEQ, one_batch, 0)

    def rope256(z, rows):
        cos, sin = t256_ref[0, rows, :], t256_ref[1, rows, :]
        parts = []
        for hh in range(_PROJ_NC // RET_HEAD_DIM):
            x1 = z[:, hh * RET_HEAD_DIM:hh * RET_HEAD_DIM + LANES]
            x2 = z[:, hh * RET_HEAD_DIM + LANES:(hh + 1) * RET_HEAD_DIM]
            parts += [x1 * cos - x2 * sin, x2 * cos + x1 * sin]
        return jnp.concatenate(parts, axis=1)

    def rope128(z, rows):
        cos_full, sin_signed = t128_ref[0, rows, :], t128_ref[1, rows, :]
        halves = []
        for hh in range(_PROJ_NC // LANES):
            x = z[:, hh * LANES:(hh + 1) * LANES]
            halves.append(x * cos_full + pltpu.roll(x, LANES // 2, 1) * sin_signed)
        return jnp.concatenate(halves, axis=1)

    epilogues = {
        _EPI_ROPE256: rope256,
        _EPI_NONE: lambda z, rows: z,
        _EPI_SILU: lambda z, rows: z * jax.nn.sigmoid(z),
        _EPI_SIGMOID: lambda z, rows: jax.nn.sigmoid(z),
        _EPI_ROPE128: rope128,
    }
    for k, fn in epilogues.items():
        pl.when(kind == k)(functools.partial(emit, fn))


def _in_proj(hs, w_in, layer, b_in, t256, t128):
    T = hs.shape[1]
    tbl = jnp.asarray(np.array(_TILE_PLAN, dtype=np.int32).T)
    assert _PROJ_TM % SEQ == 0 and SEQ % _PROJ_MC == 0
    grid_spec = pltpu.PrefetchScalarGridSpec(
        num_scalar_prefetch=1,
        grid=(T // _PROJ_TM, IN_WIDTH // _PROJ_TN),
        in_specs=[
            pl.BlockSpec((None, _PROJ_TM, D_MODEL), lambda i, j, t: (t[1, j], i, 0)),
            pl.BlockSpec((None, D_MODEL, _PROJ_TN), lambda i, j, t: (layer, 0, t[0, j])),
            pl.BlockSpec((1, _PROJ_TN), lambda i, j, t: (0, t[0, j])),
            pl.BlockSpec((None, 2, SEQ, LANES), lambda i, j, t: (t[3, j], 0, 0, 0)),
            pl.BlockSpec((None, 2, SEQ, LANES), lambda i, j, t: (t[1, j], 0, 0, 0)),
        ],
        out_specs=pl.BlockSpec((_PROJ_TM, _PROJ_TN), lambda i, j, t: (i, t[0, j])),
    )
    return pl.pallas_call(
        _in_proj_kernel,
        grid_spec=grid_spec,
        out_shape=jax.ShapeDtypeStruct((T, IN_WIDTH), BF16),
        compiler_params=_cparams(("arbitrary", "arbitrary")),
        name="in_proj",
    )(tbl, hs, w_in, b_in.reshape(1, IN_WIDTH), t256, t128)


def _rope_tables():
    pos = jnp.arange(SEQ, dtype=jnp.int32)

    def cos_sin(hd, p):
        inv = ROPE_THETA ** (-jnp.arange(0, hd, 2, dtype=F32) / hd)
        ang = p.astype(F32)[:, None] * inv[None, :]
        return jnp.cos(ang), jnp.sin(ang)

    c, s = cos_sin(RET_HEAD_DIM, pos)
    k_scale = RET_HEAD_DIM ** -0.5
    t256 = jnp.stack([jnp.stack([c, s]), jnp.stack([c * k_scale, s * k_scale])])
    layouts = []
    for d in ATT_DILATIONS:
        p = pos.reshape(SEQ // d, d).T.reshape(SEQ)
        c, s = cos_sin(ATT_HEAD_DIM, p)
        layouts.append(jnp.stack([jnp.concatenate([c, c], axis=1), jnp.concatenate([-s, s], axis=1)]))
    return t256, jnp.stack(layouts)


def _retention_kernel(q_ref, k_ref, v_ref, g_ref, nw_ref, dec_ref, qd_ref, kd_ref, cd_ref, o_ref, st_ref):
    C = RET_CHUNK
    st_ref[...] = jnp.zeros_like(st_ref)
    dec = dec_ref[...]
    qd = qd_ref[...]
    kd = kd_ref[...]
    cd = cd_ref[...]
    nw = nw_ref[...]
    for n in range(SEQ // C):
        rows = slice(n * C, (n + 1) * C)
        q = q_ref[rows, :]
        k = k_ref[rows, :]
        v = v_ref[rows, :]
        s = lax.dot_general(q, k, (((1,), (1,)), ((), ())), preferred_element_type=F32) * dec
        intra = jnp.dot(s.astype(BF16), v, preferred_element_type=F32)
        state = st_ref[...]
        inter = jnp.dot(q, state.astype(BF16), preferred_element_type=F32) * qd
        kdec = (k.astype(F32) * kd).astype(BF16)
        kv = lax.dot_general(kdec, v, (((0,), (0,)), ((), ())), preferred_element_type=F32)
        st_ref[...] = state * cd + kv
        r = intra + inter
        mu = jnp.mean(r, axis=-1, keepdims=True)
        rc = r - mu
        var = jnp.mean(rc * rc, axis=-1, keepdims=True)
        rn = rc * lax.rsqrt(var + HEAD_NORM_EPS)
        gate = g_ref[rows, :].astype(F32)
        o_ref[rows, :] = (rn * nw * (gate * jax.nn.sigmoid(gate))).astype(BF16)


def _retention_tables():
    C = RET_CHUNK
    lg = jnp.log(1.0 - 2.0 ** (-5.0 - jnp.arange(RET_HEADS, dtype=F32)))
    idx = jnp.arange(C, dtype=F32)
    diff = idx[:, None] - idx[None, :]
    intra = jnp.where(diff[None] >= 0, jnp.exp(jnp.maximum(diff, 0.0)[None] * lg[:, None, None]), 0.0)
    q_decay = jnp.exp((idx + 1.0)[None, :] * lg[:, None])
    k_decay = jnp.exp((C - 1.0 - idx)[None, :] * lg[:, None])
    chunk_decay = jnp.exp(C * lg)
    bcast = (RET_HEADS, C, RET_HEAD_DIM)
    return (intra.astype(F32),
            jnp.broadcast_to(q_decay[:, :, None], bcast).astype(F32),
            jnp.broadcast_to(k_decay[:, :, None], bcast).astype(F32),
            jnp.broadcast_to(chunk_decay[:, None, None], (RET_HEADS, 1, RET_HEAD_DIM)).astype(F32))


def _retention(z, ret_norm_w, tables):
    T = z.shape[0]
    hd = RET_HEAD_DIM
    per_head = RET_HEADS
    dec, qd, kd, cd = tables

    def zspec(col_tile):
        return pl.BlockSpec((SEQ, hd), lambda b, h: (b, col_tile * per_head + h))

    def tspec(shape):
        return pl.BlockSpec((None,) + shape, lambda b, h: (h, 0, 0))

    return pl.pallas_call(
        _retention_kernel,
        grid=(T // SEQ, RET_HEADS),
        in_specs=[zspec(_COL_RQ), zspec(_COL_RK), zspec(_COL_RV), zspec(_COL_RG),
                  pl.BlockSpec((1, hd), lambda b, h: (0, h)),
                  tspec((RET_CHUNK, RET_CHUNK)), tspec((RET_CHUNK, hd)), tspec((RET_CHUNK, hd)),
                  tspec((1, hd))],
        out_specs=pl.BlockSpec((SEQ, hd), lambda b, h: (b, h)),
        out_shape=jax.ShapeDtypeStruct((T, RET_HEADS * hd), BF16),
        scratch_shapes=[pltpu.VMEM((hd, hd), F32)],
        compiler_params=_cparams(("arbitrary", "arbitrary")),
        name="retention",
    )(z, z, z, z, ret_norm_w.reshape(1, RET_HEADS * hd), dec, qd, kd, cd)


_ATT_HEADS_PER_STEP = 2

def _attention_kernel(q0, k0, v0, q1, k1, v1, q2, k2, v2, g_ref, o_ref,
                      n2_ref, d2_ref, m2_ref, n12_ref, d12_ref, m12_ref, s_ref, p_ref, m_ref):
    blk = ATT_BLOCK
    hd = ATT_HEAD_DIM
    n_blk = SEQ // blk
    scale = hd ** -0.5
    neg_inf = jnp.float32(-jnp.inf)
    row = lax.broadcasted_iota(jnp.int32, (blk, 2 * blk), 0)
    col = lax.broadcasted_iota(jnp.int32, (blk, 2 * blk), 1)
    in_prev = col < blk
    valid = jnp.where(in_prev, -1, 1) * (row - jnp.where(in_prev, col, col - blk)) >= 0
    bias = jnp.where(valid, jnp.float32(0.0), neg_inf)
    contract_last = (((1,), (1,)), ((), ()))

    def group(d, q_ref, k_ref, v_ref, cs, finish):
        per_seq = n_blk // d
        for bi in range(n_blk):
            q = q_ref[bi * blk:(bi + 1) * blk, cs]
            if bi % per_seq == 0:
                s_ref[bi, :, :blk] = jnp.full((blk, blk), neg_inf, F32)
                s_ref[bi, :, blk:] = lax.dot_general(q, k_ref[bi * blk:(bi + 1) * blk, cs], contract_last,
                                                     preferred_element_type=F32)
            else:
                s_ref[bi] = lax.dot_general(q, k_ref[(bi - 1) * blk:(bi + 1) * blk, cs], contract_last,
                                            preferred_element_type=F32)
        s = s_ref[...] * scale + bias[None]
        m = jnp.max(s, axis=-1, keepdims=True)
        p_ref[...] = jnp.exp(s - m).astype(BF16)
        m_ref[...] = jnp.broadcast_to(m, (n_blk, blk, hd))
        for bi in range(n_blk):
            first = bi % per_seq == 0
            vv = v_ref[(bi if first else bi - 1) * blk:(bi + 1) * blk, cs]
            v_aug = jnp.concatenate([vv, jnp.ones_like(vv)], axis=1)
            pv = jnp.dot(p_ref[bi, :, blk:] if first else p_ref[bi], v_aug, preferred_element_type=F32)
            finish(bi, pv[:, :hd], pv[:, hd:], m_ref[bi])

    d1, d4, d16 = ATT_DILATIONS
    assert (d1, d4 * d4) == (1, d16)
    rows_by4 = SEQ // d4

    def finish_d16(bi, num, den, m):
        dst = pl.ds((bi % d4) * rows_by4 + bi // d4, blk, stride=d4)
        n2_ref[dst, :] = num
        d2_ref[dst, :] = den
        m2_ref[dst, :] = m

    def finish_d4(bi, num, den, m):
        rows = slice(bi * blk, (bi + 1) * blk)
        m2 = m2_ref[rows, :]
        m12 = jnp.maximum(m, m2)
        w1, w2 = jnp.exp(m - m12), jnp.exp(m2 - m12)
        r, i = divmod(bi, rows_by4 // blk)
        dst = pl.ds(i * blk * d4 + r, blk, stride=d4)
        n12_ref[dst, :] = w1 * num + w2 * n2_ref[rows, :]
        d12_ref[dst, :] = w1 * den + w2 * d2_ref[rows, :]
        m12_ref[dst, :] = m12

    def finish_d1(cs, bi, num, den, m):
        rows = slice(bi * blk, (bi + 1) * blk)
        m12 = m12_ref[rows, :]
        m_all = jnp.maximum(m, m12)
        w0, w12 = jnp.exp(m - m_all), jnp.exp(m12 - m_all)
        a = (w0 * num + w12 * n12_ref[rows, :]) / (w0 * den + w12 * d12_ref[rows, :])
        gate = g_ref[rows, cs].astype(F32)
        o_ref[rows, cs] = (a * (gate * jax.nn.sigmoid(gate))).astype(BF16)

    for head in range(_ATT_HEADS_PER_STEP):
        cs = slice(head * hd, (head + 1) * hd)
        group(d16, q2, k2, v2, cs, finish_d16)
        group(d4, q1, k1, v1, cs, finish_d4)
        group(d1, q0, k0, v0, cs, functools.partial(finish_d1, cs))


def _attention(z):
    T = z.shape[0]
    hd = ATT_HEAD_DIM
    width = _ATT_HEADS_PER_STEP * hd
    per_tile = _PROJ_TN // width

    def zspec(col_tile):
        return pl.BlockSpec((SEQ, width), lambda b, h: (b, col_tile * per_tile + h))

    in_specs = []
    for g in range(len(ATT_DILATIONS)):
        in_specs += [zspec(_COL_AQ + g), zspec(_COL_AK + g), zspec(_COL_AV + g)]
    in_specs.append(zspec(_COL_AG))
    return pl.pallas_call(
        _attention_kernel,
        grid=(T // SEQ, ATT_HEADS_PER_GROUP // _ATT_HEADS_PER_STEP),
        in_specs=in_specs,
        out_specs=pl.BlockSpec((SEQ, width), lambda b, h: (b, h)),
        out_shape=jax.ShapeDtypeStruct((T, ATT_HEADS_PER_GROUP * hd), BF16),
        scratch_shapes=[pltpu.VMEM((SEQ, hd), F32)] * 6 + [
                        pltpu.VMEM((SEQ // ATT_BLOCK, ATT_BLOCK, 2 * ATT_BLOCK), F32),
                        pltpu.VMEM((SEQ // ATT_BLOCK, ATT_BLOCK, 2 * ATT_BLOCK), BF16),
                        pltpu.VMEM((SEQ // ATT_BLOCK, ATT_BLOCK, hd), F32)],
        compiler_params=_cparams(("arbitrary", "arbitrary")),
        name="attention",
    )(*([z] * 10))


_CONV_ROWS = 1024
_CONV_HALO = 32
_CONV_NORM_ROWS = 256


def _conv_kernel(cl_ref, cs_ref, g_ref, dw_ref, dwb_ref, lnw_ref, lnb_ref, o_ref, pad_ref, acc_ref):
    R, H = _CONV_ROWS, _CONV_HALO
    n_slab = D_MODEL // LANES

    @pl.when(pl.program_id(1) == 0)
    def _():
        pad_ref[:, 0:H, :] = jnp.zeros((n_slab, H, LANES), F32)

    lead = H - (CONV_TAPS - 1)
    for c in range(n_slab):
        cols = slice(c * LANES, (c + 1) * LANES)
        pad_ref[c, H:H + R, :] = cl_ref[:, cols].astype(F32) * cs_ref[:, cols].astype(F32)
    for c in range(n_slab):
        cols = slice(c * LANES, (c + 1) * LANES)
        acc = pad_ref[c, lead:lead + R, :] * dw_ref[0:1, cols]
        for k in range(1, CONV_TAPS):
            acc = acc + pad_ref[c, lead + k:lead + k + R, :] * dw_ref[k:k + 1, cols]
        acc_ref[:, cols] = acc
    for c in range(n_slab):
        pad_ref[c, 0:H, :] = pad_ref[c, R:R + H, :]

    def norm_rows(i, carry):
        rows = pl.ds(pl.multiple_of(i * _CONV_NORM_ROWS, _CONV_NORM_ROWS), _CONV_NORM_ROWS)
        c = acc_ref[rows, :] + dwb_ref[...]
        mu = jnp.mean(c, axis=-1, keepdims=True)
        cc = c - mu
        var = jnp.mean(cc * cc, axis=-1, keepdims=True)
        y = cc * lax.rsqrt(var + HEAD_NORM_EPS) * lnw_ref[...] + lnb_ref[...]
        y = y * jax.nn.sigmoid(y)
        o_ref[rows, :] = (y * g_ref[rows, :].astype(F32)).astype(BF16)
        return carry

    lax.fori_loop(0, R // _CONV_NORM_ROWS, norm_rows, 0)


def _conv(z, dw_w, dw_b, ln_w, ln_b):
    T = z.shape[0]
    R = _CONV_ROWS
    n_r = SEQ // R

    def zspec(col_tile):
        return pl.BlockSpec((R, D_MODEL), lambda b, j: (b * n_r + j, col_tile))

    def vec():
        return pl.BlockSpec((1, D_MODEL), lambda b, j: (0, 0))

    return pl.pallas_call(
        _conv_kernel,
        grid=(T // SEQ, n_r),
        in_specs=[zspec(_COL_CLIN), zspec(_COL_CGATE), zspec(_COL_CG),
                  pl.BlockSpec((CONV_TAPS, D_MODEL), lambda b, j: (0, 0)), vec(), vec(), vec()],
        out_specs=pl.BlockSpec((R, D_MODEL), lambda b, j: (b * n_r + j, 0)),
        out_shape=jax.ShapeDtypeStruct((T, D_MODEL), BF16),
        scratch_shapes=[pltpu.VMEM((D_MODEL // LANES, R + _CONV_HALO, LANES), F32),
                        pltpu.VMEM((R, D_MODEL), F32)],
        compiler_params=_cparams(("arbitrary", "arbitrary")),
        name="conv",
    )(z, z, z, dw_w, dw_b.reshape(1, D_MODEL), ln_w.reshape(1, D_MODEL), ln_b.reshape(1, D_MODEL))


_OUT_TM = 512


def _out_proj_kernel(final, r_ref, a_ref, c_ref, g0_ref, g1_ref, g2_ref, x_ref,
                     wr_ref, wa_ref, wc_ref, wo_ref, fw_ref, o_ref):
    def branch(in_ref, w_ref, gate_ref):
        y = jnp.dot(in_ref[...], w_ref[...], preferred_element_type=F32)
        return jax.nn.sigmoid(gate_ref[...].astype(F32)) * y

    merged = branch(r_ref, wr_ref, g0_ref) + branch(a_ref, wa_ref, g1_ref) + branch(c_ref, wc_ref, g2_ref)
    out = x_ref[...] + jnp.dot(merged.astype(BF16), wo_ref[...], preferred_element_type=F32)
    if final:
        out = out * lax.rsqrt(jnp.mean(out * out, axis=-1, keepdims=True) + NORM_EPS) * fw_ref[...]
    o_ref[...] = out


def _out_proj(rg, ag, cg, z, xf, w_r, w_a, w_c, w_o, final_w, final):
    T = xf.shape[0]
    tm = _OUT_TM

    def rows(col_tile=0):
        return pl.BlockSpec((tm, D_MODEL), lambda i: (i, col_tile))

    def weight():
        return pl.BlockSpec((D_MODEL, D_MODEL), lambda i: (0, 0))

    return pl.pallas_call(
        functools.partial(_out_proj_kernel, final),
        grid=(T // tm,),
        in_specs=[rows(), rows(), rows(), rows(_COL_MG), rows(_COL_MG + 1), rows(_COL_MG + 2), rows(),
                  weight(), weight(), weight(), weight(),
                  pl.BlockSpec((1, D_MODEL), lambda i: (0, 0))],
        out_specs=rows(),
        out_shape=jax.ShapeDtypeStruct((T, D_MODEL), F32),
        compiler_params=_cparams(("arbitrary",)),
        name="out_proj_final" if final else "out_proj",
    )(rg, ag, cg, z, z, z, xf, w_r, w_a, w_c, w_o, final_w.reshape(1, D_MODEL))


def kernel(x, norm_w, w_in, b_in, ret_norm_w, ret_w_o, att_w_o, conv_dw_w, conv_dw_b,
           conv_norm_w, conv_norm_b, conv_w_o, w_out, final_norm_w):
    B, S, D = x.shape
    assert (S, D) == (SEQ, D_MODEL) and w_in.shape[-1] == IN_WIDTH
    depth = w_in.shape[0]
    t256, t128 = _rope_tables()
    ret_tables = _retention_tables()
    xf = x.reshape(B * S, D)
    for l in range(depth):
        hs = _norm_perm(xf, norm_w[l])
        z = _in_proj(hs, w_in, l, b_in[l], t256, t128)
        rg = _retention(z, ret_norm_w[l], ret_tables)
        ag = _attention(z)
        cg = _conv(z, conv_dw_w[l], conv_dw_b[l], conv_norm_w[l], conv_norm_b[l])
        xf = _out_proj(rg, ag, cg, z, xf,
                       ret_w_o[l].astype(BF16), att_w_o[l].astype(BF16), conv_w_o[l].astype(BF16),
                       w_out[l].astype(BF16), final_norm_w, final=(l == depth - 1))
    return xf.reshape(B, S, D)
```

```python
import functools

import numpy as np
import jax
import jax.numpy as jnp
from jax import lax
from jax.experimental import pallas as pl
from jax.experimental.pallas import tpu as pltpu

F32 = jnp.float32
BF16 = jnp.bfloat16

D_MODEL = 1024
SEQ = 2048
ROPE_THETA = 10000.0
NORM_EPS = 1e-6
HEAD_NORM_EPS = 1e-5
RET_HEADS = 4
RET_HEAD_DIM = 256
RET_CHUNK = 256
ATT_DILATIONS = (1, 4, 16)
ATT_HEADS_PER_GROUP = 8
ATT_HEAD_DIM = 128
ATT_BLOCK = 128
CONV_TAPS = 31
IN_WIDTH = 20480

LANES = 128
SUBLANES = 8
VMEM_LIMIT = 56 * 1024 * 1024

_COL_RQ, _COL_RK, _COL_RV, _COL_RG = 0, 1, 2, 3
_COL_AQ, _COL_AK, _COL_AV, _COL_AG = 4, 7, 10, 13
_COL_CLIN, _COL_CGATE, _COL_CG, _COL_MG = 14, 15, 16, 17

_EPI_ROPE256, _EPI_NONE, _EPI_SILU, _EPI_SIGMOID, _EPI_ROPE128 = 0, 1, 2, 3, 4

_TILE_PLAN = (
    (_COL_RQ, 0, _EPI_ROPE256, 0), (_COL_RK, 0, _EPI_ROPE256, 1),
    (_COL_RV, 0, _EPI_NONE, 0), (_COL_RG, 0, _EPI_NONE, 0),
    (_COL_AQ, 0, _EPI_ROPE128, 0), (_COL_AK, 0, _EPI_ROPE128, 0), (_COL_AV, 0, _EPI_NONE, 0),
    (_COL_AG, 0, _EPI_NONE, 0), (_COL_CLIN, 0, _EPI_NONE, 0), (_COL_CGATE, 0, _EPI_SIGMOID, 0),
    (_COL_CG, 0, _EPI_SILU, 0),
    (_COL_MG, 0, _EPI_NONE, 0), (_COL_MG + 1, 0, _EPI_NONE, 0), (_COL_MG + 2, 0, _EPI_NONE, 0),
    (_COL_AQ + 1, 1, _EPI_ROPE128, 0), (_COL_AK + 1, 1, _EPI_ROPE128, 0), (_COL_AV + 1, 1, _EPI_NONE, 0),
    (_COL_AQ + 2, 2, _EPI_ROPE128, 0), (_COL_AK + 2, 2, _EPI_ROPE128, 0), (_COL_AV + 2, 2, _EPI_NONE, 0),
)


def _cparams(sem):
    return pltpu.CompilerParams(dimension_semantics=sem, vmem_limit_bytes=VMEM_LIMIT)


_NORM_SLAB_BUFS = 2


def _norm_perm_kernel(x_ref, w_ref, hs_ref, slab_ref, slab4_ref):
    d4, d16 = ATT_DILATIONS[1:]
    assert d4 * d4 == d16
    L4, L16 = SEQ // d4, SEQ // d16
    x = x_ref[...]
    inv = lax.rsqrt(jnp.mean(x * x, axis=-1, keepdims=True) + NORM_EPS)
    for c in range(D_MODEL // LANES):
        cols = slice(c * LANES, (c + 1) * LANES)
        buf = c % _NORM_SLAB_BUFS
        h = x_ref[:, cols] * inv * w_ref[:, cols]
        slab_ref[buf] = h
        hs_ref[0, :, cols] = h.astype(BF16)
        for r in range(d4):
            v = slab_ref[buf, pl.ds(r, L4, stride=d4), :]
            slab4_ref[buf, r * L4:(r + 1) * L4, :] = v
            hs_ref[1, r * L4:(r + 1) * L4, cols] = v.astype(BF16)
        for r in range(d16):
            v = slab4_ref[buf, pl.ds((r % d4) * L4 + r // d4, L16, stride=d4), :]
            hs_ref[2, r * L16:(r + 1) * L16, cols] = v.astype(BF16)


def _norm_perm(xf, norm_w):
    T = xf.shape[0]
    slab = pltpu.VMEM((_NORM_SLAB_BUFS, SEQ, LANES), F32)
    return pl.pallas_call(
        _norm_perm_kernel,
        grid=(T // SEQ,),
        in_specs=[pl.BlockSpec((SEQ, D_MODEL), lambda b: (b, 0)),
                  pl.BlockSpec((1, D_MODEL), lambda b: (0, 0))],
        out_specs=pl.BlockSpec((3, SEQ, D_MODEL), lambda b: (0, b, 0)),
        out_shape=jax.ShapeDtypeStruct((3, T, D_MODEL), BF16),
        scratch_shapes=[slab, slab],
        compiler_params=_cparams(("arbitrary",)),
        name="norm_perm",
    )(xf, norm_w.reshape(1, D_MODEL))


_PROJ_TM = 4096
_PROJ_TN = 1024
_PROJ_NC = 256
_PROJ_MC = 2048


def _in_proj_kernel(tbl_ref, hs_ref, w_ref, b_ref, t256_ref, t128_ref, o_ref):
    kind = tbl_ref[2, pl.program_id(1)]

    def emit(epilogue):
        def one_batch(bt, carry):
            base = pl.multiple_of(bt * SEQ, SEQ)
            for nc in range(_PROJ_TN // _PROJ_NC):
                cols = slice(nc * _PROJ_NC, (nc + 1) * _PROJ_NC)
                w = w_ref[:, cols].astype(BF16)
                for mc in range(SEQ // _PROJ_MC):
                    pos = slice(mc * _PROJ_MC, (mc + 1) * _PROJ_MC)
                    rows = pl.ds(base + mc * _PROJ_MC, _PROJ_MC)
                    z = jnp.dot(hs_ref[rows, :], w, preferred_element_type=F32) + b_ref[:, cols]
                    o_ref[rows, cols] = epilogue(z, pos).astype(BF16)
            return carry

        lax.fori_loop(0, _PROJ_TM // SEQ, one_batch, 0)

    def rope256(z, rows):
        cos, sin = t256_ref[0, rows, :], t256_ref[1, rows, :]
        parts = []
        for hh in range(_PROJ_NC // RET_HEAD_DIM):
            x1 = z[:, hh * RET_HEAD_DIM:hh * RET_HEAD_DIM + LANES]
            x2 = z[:, hh * RET_HEAD_DIM + LANES:(hh + 1) * RET_HEAD_DIM]
            parts += [x1 * cos - x2 * sin, x2 * cos + x1 * sin]
        return jnp.concatenate(parts, axis=1)

    def rope128(z, rows):
        cos_full, sin_signed = t128_ref[0, rows, :], t128_ref[1, rows, :]
        halves = []
        for hh in range(_PROJ_NC // LANES):
            x = z[:, hh * LANES:(hh + 1) * LANES]
            halves.append(x * cos_full + pltpu.roll(x, LANES // 2, 1) * sin_signed)
        return jnp.concatenate(halves, axis=1)

    epilogues = {
        _EPI_ROPE256: rope256,
        _EPI_NONE: lambda z, rows: z,
        _EPI_SILU: lambda z, rows: z * jax.nn.sigmoid(z),
        _EPI_SIGMOID: lambda z, rows: jax.nn.sigmoid(z),
        _EPI_ROPE128: rope128,
    }
    for k, fn in epilogues.items():
        pl.when(kind == k)(functools.partial(emit, fn))


def _in_proj(hs, w_in, layer, b_in, t256, t128):
    T = hs.shape[1]
    tbl = jnp.asarray(np.array(_TILE_PLAN, dtype=np.int32).T)
    assert _PROJ_TM % SEQ == 0 and SEQ % _PROJ_MC == 0
    grid_spec = pltpu.PrefetchScalarGridSpec(
        num_scalar_prefetch=1,
        grid=(T // _PROJ_TM, IN_WIDTH // _PROJ_TN),
        in_specs=[
            pl.BlockSpec((None, _PROJ_TM, D_MODEL), lambda i, j, t: (t[1, j], i, 0)),
            pl.BlockSpec((None, D_MODEL, _PROJ_TN), lambda i, j, t: (layer, 0, t[0, j])),
            pl.BlockSpec((1, _PROJ_TN), lambda i, j, t: (0, t[0, j])),
            pl.BlockSpec((None, 2, SEQ, LANES), lambda i, j, t: (t[3, j], 0, 0, 0)),
            pl.BlockSpec((None, 2, SEQ, LANES), lambda i, j, t: (t[1, j], 0, 0, 0)),
        ],
        out_specs=pl.BlockSpec((_PROJ_TM, _PROJ_TN), lambda i, j, t: (i, t[0, j])),
    )
    return pl.pallas_call(
        _in_proj_kernel,
        grid_spec=grid_spec,
        out_shape=jax.ShapeDtypeStruct((T, IN_WIDTH), BF16),
        compiler_params=_cparams(("arbitrary", "arbitrary")),
        name="in_proj",
    )(tbl, hs, w_in, b_in.reshape(1, IN_WIDTH), t256, t128)


def _rope_tables():
    pos = jnp.arange(SEQ, dtype=jnp.int32)

    def cos_sin(hd, p):
        inv = ROPE_THETA ** (-jnp.arange(0, hd, 2, dtype=F32) / hd)
        ang = p.astype(F32)[:, None] * inv[None, :]
        return jnp.cos(ang), jnp.sin(ang)

    c, s = cos_sin(RET_HEAD_DIM, pos)
    k_scale = RET_HEAD_DIM ** -0.5
    t256 = jnp.stack([jnp.stack([c, s]), jnp.stack([c * k_scale, s * k_scale])])
    layouts = []
    for d in ATT_DILATIONS:
        p = pos.reshape(SEQ // d, d).T.reshape(SEQ)
        c, s = cos_sin(ATT_HEAD_DIM, p)
        layouts.append(jnp.stack([jnp.concatenate([c, c], axis=1), jnp.concatenate([-s, s], axis=1)]))
    return t256, jnp.stack(layouts)


_RET_HEADS_PER_STEP = 2


def _retention_kernel(q_ref, k_ref, v_ref, g_ref, nw_ref, dec_ref, qd_ref, kd_ref, cd_ref, o_ref, st_ref):
    C = RET_CHUNK
    hd = RET_HEAD_DIM
    st_ref[...] = jnp.zeros_like(st_ref)
    for n in range(SEQ // C):
        rows = slice(n * C, (n + 1) * C)
        for head in range(_RET_HEADS_PER_STEP):
            cs = slice(head * hd, (head + 1) * hd)
            q = q_ref[rows, cs]
            k = k_ref[rows, cs]
            v = v_ref[rows, cs]
            s = lax.dot_general(q, k, (((1,), (1,)), ((), ())), preferred_element_type=F32) * dec_ref[head]
            intra = jnp.dot(s.astype(BF16), v, preferred_element_type=F32)
            state = st_ref[head]
            inter = jnp.dot(q, state.astype(BF16), preferred_element_type=F32) * qd_ref[head]
            kdec = (k.astype(F32) * kd_ref[head]).astype(BF16)
            kv = lax.dot_general(kdec, v, (((0,), (0,)), ((), ())), preferred_element_type=F32)
            st_ref[head] = state * cd_ref[head] + kv
            r = intra + inter
            mu = jnp.mean(r, axis=-1, keepdims=True)
            rc = r - mu
            var = jnp.mean(rc * rc, axis=-1, keepdims=True)
            rn = rc * lax.rsqrt(var + HEAD_NORM_EPS)
            gate = g_ref[rows, cs].astype(F32)
            o_ref[rows, cs] = (rn * nw_ref[:, cs] * (gate * jax.nn.sigmoid(gate))).astype(BF16)


def _retention_tables():
    C = RET_CHUNK
    lg = jnp.log(1.0 - 2.0 ** (-5.0 - jnp.arange(RET_HEADS, dtype=F32)))
    idx = jnp.arange(C, dtype=F32)
    diff = idx[:, None] - idx[None, :]
    intra = jnp.where(diff[None] >= 0, jnp.exp(jnp.maximum(diff, 0.0)[None] * lg[:, None, None]), 0.0)
    q_decay = jnp.exp((idx + 1.0)[None, :] * lg[:, None])
    k_decay = jnp.exp((C - 1.0 - idx)[None, :] * lg[:, None])
    chunk_decay = jnp.exp(C * lg)
    bcast = (RET_HEADS, C, RET_HEAD_DIM)
    return (intra.astype(F32),
            jnp.broadcast_to(q_decay[:, :, None], bcast).astype(F32),
            jnp.broadcast_to(k_decay[:, :, None], bcast).astype(F32),
            jnp.broadcast_to(chunk_decay[:, None, None], (RET_HEADS, 1, RET_HEAD_DIM)).astype(F32))


def _retention(z, ret_norm_w, tables):
    T = z.shape[0]
    hd = RET_HEAD_DIM
    n_h = _RET_HEADS_PER_STEP
    width = n_h * hd
    per_tile = _PROJ_TN // width
    dec, qd, kd, cd = tables

    def zspec(col_tile):
        return pl.BlockSpec((SEQ, width), lambda b, h: (b, col_tile * per_tile + h))

    def tspec(shape):
        return pl.BlockSpec((n_h,) + shape, lambda b, h: (h, 0, 0))

    return pl.pallas_call(
        _retention_kernel,
        grid=(T // SEQ, RET_HEADS // n_h),
        in_specs=[zspec(_COL_RQ), zspec(_COL_RK), zspec(_COL_RV), zspec(_COL_RG),
                  pl.BlockSpec((1, width), lambda b, h: (0, h)),
                  tspec((RET_CHUNK, RET_CHUNK)), tspec((RET_CHUNK, hd)), tspec((RET_CHUNK, hd)),
                  tspec((1, hd))],
        out_specs=pl.BlockSpec((SEQ, width), lambda b, h: (b, h)),
        out_shape=jax.ShapeDtypeStruct((T, RET_HEADS * hd), BF16),
        scratch_shapes=[pltpu.VMEM((n_h, hd, hd), F32)],
        compiler_params=_cparams(("arbitrary", "arbitrary")),
        name="retention",
    )(z, z, z, z, ret_norm_w.reshape(1, RET_HEADS * hd), dec, qd, kd, cd)


_ATT_HEADS_PER_STEP = 2

def _attention_kernel(q0, k0, v0, q1, k1, v1, q2, k2, v2, g_ref, o_ref,
                      n2_ref, d2_ref, m2_ref, n12_ref, d12_ref, m12_ref, s_ref, p_ref, m_ref):
    blk = ATT_BLOCK
    hd = ATT_HEAD_DIM
    n_blk = SEQ // blk
    scale = hd ** -0.5
    neg_inf = jnp.float32(-jnp.inf)
    row = lax.broadcasted_iota(jnp.int32, (blk, 2 * blk), 0)
    col = lax.broadcasted_iota(jnp.int32, (blk, 2 * blk), 1)
    in_prev = col < blk
    valid = jnp.where(in_prev, -1, 1) * (row - jnp.where(in_prev, col, col - blk)) >= 0
    bias = jnp.where(valid, jnp.float32(0.0), neg_inf)
    contract_last = (((1,), (1,)), ((), ()))

    def group(d, q_ref, k_ref, v_ref, cs, finish):
        per_seq = n_blk // d
        for bi in range(n_blk):
            q = q_ref[bi * blk:(bi + 1) * blk, cs]
            if bi % per_seq == 0:
                s_ref[bi, :, :blk] = jnp.full((blk, blk), neg_inf, F32)
                s_ref[bi, :, blk:] = lax.dot_general(q, k_ref[bi * blk:(bi + 1) * blk, cs], contract_last,
                                                     preferred_element_type=F32)
            else:
                s_ref[bi] = lax.dot_general(q, k_ref[(bi - 1) * blk:(bi + 1) * blk, cs], contract_last,
                                            preferred_element_type=F32)
        s = s_ref[...] * scale + bias[None]
        m = jnp.max(s, axis=-1, keepdims=True)
        p_ref[...] = jnp.exp(s - m).astype(BF16)
        m_ref[...] = jnp.broadcast_to(m, (n_blk, blk, hd))
        for bi in range(n_blk):
            first = bi % per_seq == 0
            vv = v_ref[(bi if first else bi - 1) * blk:(bi + 1) * blk, cs]
            v_aug = jnp.concatenate([vv, jnp.ones_like(vv)], axis=1)
            pv = jnp.dot(p_ref[bi, :, blk:] if first else p_ref[bi], v_aug, preferred_element_type=F32)
            finish(bi, pv[:, :hd], pv[:, hd:], m_ref[bi])

    d1, d4, d16 = ATT_DILATIONS
    assert (d1, d4 * d4) == (1, d16)
    rows_by4 = SEQ // d4

    def finish_d16(bi, num, den, m):
        dst = pl.ds((bi % d4) * rows_by4 + bi // d4, blk, stride=d4)
        n2_ref[dst, :] = num
        d2_ref[dst, :] = den
        m2_ref[dst, :] = m

    def finish_d4(bi, num, den, m):
        rows = slice(bi * blk, (bi + 1) * blk)
        m2 = m2_ref[rows, :]
        m12 = jnp.maximum(m, m2)
        w1, w2 = jnp.exp(m - m12), jnp.exp(m2 - m12)
        r, i = divmod(bi, rows_by4 // blk)
        dst = pl.ds(i * blk * d4 + r, blk, stride=d4)
        n12_ref[dst, :] = w1 * num + w2 * n2_ref[rows, :]
        d12_ref[dst, :] = w1 * den + w2 * d2_ref[rows, :]
        m12_ref[dst, :] = m12

    def finish_d1(cs, bi, num, den, m):
        rows = slice(bi * blk, (bi + 1) * blk)
        m12 = m12_ref[rows, :]
        m_all = jnp.maximum(m, m12)
        w0, w12 = jnp.exp(m - m_all), jnp.exp(m12 - m_all)
        a = (w0 * num + w12 * n12_ref[rows, :]) / (w0 * den + w12 * d12_ref[rows, :])
        gate = g_ref[rows, cs].astype(F32)
        o_ref[rows, cs] = (a * (gate * jax.nn.sigmoid(gate))).astype(BF16)

    for head in range(_ATT_HEADS_PER_STEP):
        cs = slice(head * hd, (head + 1) * hd)
        group(d16, q2, k2, v2, cs, finish_d16)
        group(d4, q1, k1, v1, cs, finish_d4)
        group(d1, q0, k0, v0, cs, functools.partial(finish_d1, cs))


def _attention(z):
    T = z.shape[0]
    hd = ATT_HEAD_DIM
    width = _ATT_HEADS_PER_STEP * hd
    per_tile = _PROJ_TN // width

    def zspec(col_tile):
        return pl.BlockSpec((SEQ, width), lambda b, h: (b, col_tile * per_tile + h))

    in_specs = []
    for g in range(len(ATT_DILATIONS)):
        in_specs += [zspec(_COL_AQ + g), zspec(_COL_AK + g), zspec(_COL_AV + g)]
    in_specs.append(zspec(_COL_AG))
    return pl.pallas_call(
        _attention_kernel,
        grid=(T // SEQ, ATT_HEADS_PER_GROUP // _ATT_HEADS_PER_STEP),
        in_specs=in_specs,
        out_specs=pl.BlockSpec((SEQ, width), lambda b, h: (b, h)),
        out_shape=jax.ShapeDtypeStruct((T, ATT_HEADS_PER_GROUP * hd), BF16),
        scratch_shapes=[pltpu.VMEM((SEQ, hd), F32)] * 6 + [
                        pltpu.VMEM((SEQ // ATT_BLOCK, ATT_BLOCK, 2 * ATT_BLOCK), F32),
                        pltpu.VMEM((SEQ // ATT_BLOCK, ATT_BLOCK, 2 * ATT_BLOCK), BF16),
                        pltpu.VMEM((SEQ // ATT_BLOCK, ATT_BLOCK, hd), F32)],
        compiler_params=_cparams(("arbitrary", "arbitrary")),
        name="attention",
    )(*([z] * 10))


_CONV_ROWS = 512
_CONV_HALO = 32
_CONV_NORM_ROWS = 256


def _conv_kernel(cl_ref, cs_ref, g_ref, dw_ref, dwb_ref, lnw_ref, lnb_ref, o_ref, pad_ref, acc_ref):
    R, H = _CONV_ROWS, _CONV_HALO
    n_slab = D_MODEL // LANES

    @pl.when(pl.program_id(1) == 0)
    def _():
        pad_ref[:, 0:H, :] = jnp.zeros((n_slab, H, LANES), F32)

    lead = H - (CONV_TAPS - 1)
    for c in range(n_slab):
        cols = slice(c * LANES, (c + 1) * LANES)
        pad_ref[c, H:H + R, :] = cl_ref[:, cols].astype(F32) * cs_ref[:, cols].astype(F32)
    for c in range(n_slab):
        cols = slice(c * LANES, (c + 1) * LANES)
        acc = pad_ref[c, lead:lead + R, :] * dw_ref[0:1, cols]
        for k in range(1, CONV_TAPS):
            acc = acc + pad_ref[c, lead + k:lead + k + R, :] * dw_ref[k:k + 1, cols]
        acc_ref[:, cols] = acc
    for c in range(n_slab):
        pad_ref[c, 0:H, :] = pad_ref[c, R:R + H, :]

    def norm_rows(i, carry):
        rows = pl.ds(pl.multiple_of(i * _CONV_NORM_ROWS, _CONV_NORM_ROWS), _CONV_NORM_ROWS)
        c = acc_ref[rows, :] + dwb_ref[...]
        mu = jnp.mean(c, axis=-1, keepdims=True)
        cc = c - mu
        var = jnp.mean(cc * cc, axis=-1, keepdims=True)
        y = cc * lax.rsqrt(var + HEAD_NORM_EPS) * lnw_ref[...] + lnb_ref[...]
        y = y * jax.nn.sigmoid(y)
        o_ref[rows, :] = (y * g_ref[rows, :].astype(F32)).astype(BF16)
        return carry

    lax.fori_loop(0, R // _CONV_NORM_ROWS, norm_rows, 0)


def _conv(z, dw_w, dw_b, ln_w, ln_b):
    T = z.shape[0]
    R = _CONV_ROWS
    n_r = SEQ // R

    def zspec(col_tile):
        return pl.BlockSpec((R, D_MODEL), lambda b, j: (b * n_r + j, col_tile))

    def vec():
        return pl.BlockSpec((1, D_MODEL), lambda b, j: (0, 0))

    return pl.pallas_call(
        _conv_kernel,
        grid=(T // SEQ, n_r),
        in_specs=[zspec(_COL_CLIN), zspec(_COL_CGATE), zspec(_COL_CG),
                  pl.BlockSpec((CONV_TAPS, D_MODEL), lambda b, j: (0, 0)), vec(), vec(), vec()],
        out_specs=pl.BlockSpec((R, D_MODEL), lambda b, j: (b * n_r + j, 0)),
        out_shape=jax.ShapeDtypeStruct((T, D_MODEL), BF16),
        scratch_shapes=[pltpu.VMEM((D_MODEL // LANES, R + _CONV_HALO, LANES), F32),
                        pltpu.VMEM((R, D_MODEL), F32)],
        compiler_params=_cparams(("arbitrary", "arbitrary")),
        name="conv",
    )(z, z, z, dw_w, dw_b.reshape(1, D_MODEL), ln_w.reshape(1, D_MODEL), ln_b.reshape(1, D_MODEL))


_OUT_TM = 512


def _out_proj_kernel(final, r_ref, a_ref, c_ref, g0_ref, g1_ref, g2_ref, x_ref,
                     wr_ref, wa_ref, wc_ref, wo_ref, fw_ref, o_ref):
    def branch(in_ref, w_ref, gate_ref):
        y = jnp.dot(in_ref[...], w_ref[...], preferred_element_type=F32)
        return jax.nn.sigmoid(gate_ref[...].astype(F32)) * y

    merged = branch(r_ref, wr_ref, g0_ref) + branch(a_ref, wa_ref, g1_ref) + branch(c_ref, wc_ref, g2_ref)
    out = x_ref[...] + jnp.dot(merged.astype(BF16), wo_ref[...], preferred_element_type=F32)
    if final:
        out = out * lax.rsqrt(jnp.mean(out * out, axis=-1, keepdims=True) + NORM_EPS) * fw_ref[...]
    o_ref[...] = out


def _out_proj(rg, ag, cg, z, xf, w_r, w_a, w_c, w_o, final_w, final):
    T = xf.shape[0]
    tm = _OUT_TM

    def rows(col_tile=0):
        return pl.BlockSpec((tm, D_MODEL), lambda i: (i, col_tile))

    def weight():
        return pl.BlockSpec((D_MODEL, D_MODEL), lambda i: (0, 0))

    return pl.pallas_call(
        functools.partial(_out_proj_kernel, final),
        grid=(T // tm,),
        in_specs=[rows(), rows(), rows(), rows(_COL_MG), rows(_COL_MG + 1), rows(_COL_MG + 2), rows(),
                  weight(), weight(), weight(), weight(),
                  pl.BlockSpec((1, D_MODEL), lambda i: (0, 0))],
        out_specs=rows(),
        out_shape=jax.ShapeDtypeStruct((T, D_MODEL), F32),
        compiler_params=_cparams(("arbitrary",)),
        name="out_proj_final" if final else "out_proj",
    )(rg, ag, cg, z, z, z, xf, w_r, w_a, w_c, w_o, final_w.reshape(1, D_MODEL))


def kernel(x, norm_w, w_in, b_in, ret_norm_w, ret_w_o, att_w_o, conv_dw_w, conv_dw_b,
           conv_norm_w, conv_norm_b, conv_w_o, w_out, final_norm_w):
    B, S, D = x.shape
    assert (S, D) == (SEQ, D_MODEL) and w_in.shape[-1] == IN_WIDTH
    depth = w_in.shape[0]
    t256, t128 = _rope_tables()
    ret_tables = _retention_tables()
    xf = x.reshape(B * S, D)
    for l in range(depth):
        hs = _norm_perm(xf, norm_w[l])
        z = _in_proj(hs, w_in, l, b_in[l], t256, t128)
        rg = _retention(z, ret_norm_w[l], ret_tables)
        ag = _attention(z)
        cg = _conv(z, conv_dw_w[l], conv_dw_b[l], conv_norm_w[l], conv_norm_b[l])
        xf = _out_proj(rg, ag, cg, z, xf,
                       ret_w_o[l].astype(BF16), att_w_o[l].astype(BF16), conv_w_o[l].astype(BF16),
                       w_out[l].astype(BF16), final_norm_w, final=(l == depth - 1))
    return xf.reshape(B, S, D)
```

```python
import functools

import numpy as np
import jax
import jax.numpy as jnp
from jax import lax
from jax.experimental import pallas as pl
from jax.experimental.pallas import tpu as pltpu

F32 = jnp.float32
BF16 = jnp.bfloat16

D_MODEL = 1024
SEQ = 2048
ROPE_THETA = 10000.0
NORM_EPS = 1e-6
HEAD_NORM_EPS = 1e-5
RET_HEADS = 4
RET_HEAD_DIM = 256
RET_CHUNK = 256
ATT_DILATIONS = (1, 4, 16)
ATT_HEADS_PER_GROUP = 8
ATT_HEAD_DIM = 128
ATT_BLOCK = 128
CONV_TAPS = 31
IN_WIDTH = 20480

LANES = 128
SUBLANES = 8
VMEM_LIMIT = 56 * 1024 * 1024

_COL_RQ, _COL_RK, _COL_RV, _COL_RG = 0, 1, 2, 3
_COL_AQ, _COL_AK, _COL_AV, _COL_AG = 4, 7, 10, 13
_COL_CLIN, _COL_CGATE, _COL_CG, _COL_MG = 14, 15, 16, 17

_EPI_ROPE256, _EPI_NONE, _EPI_SILU, _EPI_SIGMOID, _EPI_ROPE128 = 0, 1, 2, 3, 4

_TILE_PLAN = (
    (_COL_RQ, 0, _EPI_ROPE256, 0), (_COL_RK, 0, _EPI_ROPE256, 1),
    (_COL_RV, 0, _EPI_NONE, 0), (_COL_RG, 0, _EPI_NONE, 0),
    (_COL_AQ, 0, _EPI_ROPE128, 0), (_COL_AK, 0, _EPI_ROPE128, 0), (_COL_AV, 0, _EPI_NONE, 0),
    (_COL_AG, 0, _EPI_NONE, 0), (_COL_CLIN, 0, _EPI_NONE, 0), (_COL_CGATE, 0, _EPI_SIGMOID, 0),
    (_COL_CG, 0, _EPI_SILU, 0),
    (_COL_MG, 0, _EPI_NONE, 0), (_COL_MG + 1, 0, _EPI_NONE, 0), (_COL_MG + 2, 0, _EPI_NONE, 0),
    (_COL_AQ + 1, 1, _EPI_ROPE128, 0), (_COL_AK + 1, 1, _EPI_ROPE128, 0), (_COL_AV + 1, 1, _EPI_NONE, 0),
    (_COL_AQ + 2, 2, _EPI_ROPE128, 0), (_COL_AK + 2, 2, _EPI_ROPE128, 0), (_COL_AV + 2, 2, _EPI_NONE, 0),
)


def _cparams(sem):
    return pltpu.CompilerParams(dimension_semantics=sem, vmem_limit_bytes=VMEM_LIMIT)


_NORM_SLAB_BUFS = 2


def _norm_perm_kernel(x_ref, w_ref, hs_ref, slab_ref, slab4_ref):
    d4, d16 = ATT_DILATIONS[1:]
    assert d4 * d4 == d16
    L4, L16 = SEQ // d4, SEQ // d16
    x = x_ref[...]
    inv = lax.rsqrt(jnp.mean(x * x, axis=-1, keepdims=True) + NORM_EPS)
    for c in range(D_MODEL // LANES):
        cols = slice(c * LANES, (c + 1) * LANES)
        buf = c % _NORM_SLAB_BUFS
        h = x_ref[:, cols] * inv * w_ref[:, cols]
        slab_ref[buf] = h
        hs_ref[0, :, cols] = h.astype(BF16)
        for r in range(d4):
            v = slab_ref[buf, pl.ds(r, L4, stride=d4), :]
            slab4_ref[buf, r * L4:(r + 1) * L4, :] = v
            hs_ref[1, r * L4:(r + 1) * L4, cols] = v.astype(BF16)
        for r in range(d16):
            v = slab4_ref[buf, pl.ds((r % d4) * L4 + r // d4, L16, stride=d4), :]
            hs_ref[2, r * L16:(r + 1) * L16, cols] = v.astype(BF16)


def _norm_perm(xf, norm_w):
    T = xf.shape[0]
    slab = pltpu.VMEM((_NORM_SLAB_BUFS, SEQ, LANES), F32)
    return pl.pallas_call(
        _norm_perm_kernel,
        grid=(T // SEQ,),
        in_specs=[pl.BlockSpec((SEQ, D_MODEL), lambda b: (b, 0)),
                  pl.BlockSpec((1, D_MODEL), lambda b: (0, 0))],
        out_specs=pl.BlockSpec((3, SEQ, D_MODEL), lambda b: (0, b, 0)),
        out_shape=jax.ShapeDtypeStruct((3, T, D_MODEL), BF16),
        scratch_shapes=[slab, slab],
        compiler_params=_cparams(("arbitrary",)),
        name="norm_perm",
    )(xf, norm_w.reshape(1, D_MODEL))


_PROJ_TM = 4096
_PROJ_TN = 1024
_PROJ_NC = 256
_PROJ_MC = 2048


def _in_proj_kernel(tbl_ref, hs_ref, w_ref, b_ref, t256_ref, t128_ref, o_ref):
    kind = tbl_ref[2, pl.program_id(1)]

    def emit(epilogue):
        def one_batch(bt, carry):
            base = pl.multiple_of(bt * SEQ, SEQ)
            for nc in range(_PROJ_TN // _PROJ_NC):
                cols = slice(nc * _PROJ_NC, (nc + 1) * _PROJ_NC)
                w = w_ref[:, cols].astype(BF16)
                for mc in range(SEQ // _PROJ_MC):
                    pos = slice(mc * _PROJ_MC, (mc + 1) * _PROJ_MC)
                    rows = pl.ds(base + mc * _PROJ_MC, _PROJ_MC)
                    z = jnp.dot(hs_ref[rows, :], w, preferred_element_type=F32) + b_ref[:, cols]
                    o_ref[rows, cols] = epilogue(z, pos).astype(BF16)
            return carry

        lax.fori_loop(0, _PROJ_TM // SEQ, one_batch, 0)

    def rope256(z, rows):
        cos, sin = t256_ref[0, rows, :], t256_ref[1, rows, :]
        parts = []
        for hh in range(_PROJ_NC // RET_HEAD_DIM):
            x1 = z[:, hh * RET_HEAD_DIM:hh * RET_HEAD_DIM + LANES]
            x2 = z[:, hh * RET_HEAD_DIM + LANES:(hh + 1) * RET_HEAD_DIM]
            parts += [x1 * cos - x2 * sin, x2 * cos + x1 * sin]
        return jnp.concatenate(parts, axis=1)

    def rope128(z, rows):
        cos_full, sin_signed = t128_ref[0, rows, :], t128_ref[1, rows, :]
        halves = []
        for hh in range(_PROJ_NC // LANES):
            x = z[:, hh * LANES:(hh + 1) * LANES]
            halves.append(x * cos_full + pltpu.roll(x, LANES // 2, 1) * sin_signed)
        return jnp.concatenate(halves, axis=1)

    epilogues = {
        _EPI_ROPE256: rope256,
        _EPI_NONE: lambda z, rows: z,
        _EPI_SILU: lambda z, rows: z * jax.nn.sigmoid(z),
        _EPI_SIGMOID: lambda z, rows: jax.nn.sigmoid(z),
        _EPI_ROPE128: rope128,
    }
    for k, fn in epilogues.items():
        pl.when(kind == k)(functools.partial(emit, fn))


def _in_proj(hs, w_in, layer, b_in, t256, t128):
    T = hs.shape[1]
    tbl = jnp.asarray(np.array(_TILE_PLAN, dtype=np.int32).T)
    assert _PROJ_TM % SEQ == 0 and SEQ % _PROJ_MC == 0
    grid_spec = pltpu.PrefetchScalarGridSpec(
        num_scalar_prefetch=1,
        grid=(T // _PROJ_TM, IN_WIDTH // _PROJ_TN),
        in_specs=[
            pl.BlockSpec((None, _PROJ_TM, D_MODEL), lambda i, j, t: (t[1, j], i, 0)),
            pl.BlockSpec((None, D_MODEL, _PROJ_TN), lambda i, j, t: (layer, 0, t[0, j])),
            pl.BlockSpec((1, _PROJ_TN), lambda i, j, t: (0, t[0, j])),
            pl.BlockSpec((None, 2, SEQ, LANES), lambda i, j, t: (t[3, j], 0, 0, 0)),
            pl.BlockSpec((None, 2, SEQ, LANES), lambda i, j, t: (t[1, j], 0, 0, 0)),
        ],
        out_specs=pl.BlockSpec((_PROJ_TM, _PROJ_TN), lambda i, j, t: (i, t[0, j])),
    )
    return pl.pallas_call(
        _in_proj_kernel,
        grid_spec=grid_spec,
        out_shape=jax.ShapeDtypeStruct((T, IN_WIDTH), BF16),
        compiler_params=_cparams(("arbitrary", "arbitrary")),
        name="in_proj",
    )(tbl, hs, w_in, b_in.reshape(1, IN_WIDTH), t256, t128)


def _rope_tables():
    pos = np.arange(SEQ, dtype=np.int32)
    f32 = np.float32

    def cos_sin(hd, p):
        inv = (f32(ROPE_THETA) ** (-np.arange(0, hd, 2, dtype=f32) / f32(hd))).astype(f32)
        ang = p.astype(f32)[:, None] * inv[None, :]
        return np.cos(ang).astype(f32), np.sin(ang).astype(f32)

    c, s = cos_sin(RET_HEAD_DIM, pos)
    k_scale = f32(RET_HEAD_DIM ** -0.5)
    t256 = np.stack([np.stack([c, s]), np.stack([c * k_scale, s * k_scale])])
    layouts = []
    for d in ATT_DILATIONS:
        p = pos.reshape(SEQ // d, d).T.reshape(SEQ)
        c, s = cos_sin(ATT_HEAD_DIM, p)
        layouts.append(np.stack([np.concatenate([c, c], axis=1), np.concatenate([-s, s], axis=1)]))
    return t256, np.stack(layouts)


_RET_HEADS_PER_STEP = 2


def _retention_kernel(q_ref, k_ref, v_ref, g_ref, nw_ref, dec_ref, qd_ref, kd_ref, cd_ref, o_ref, st_ref):
    C = RET_CHUNK
    hd = RET_HEAD_DIM
    st_ref[...] = jnp.zeros_like(st_ref)
    for n in range(SEQ // C):
        rows = slice(n * C, (n + 1) * C)
        for head in range(_RET_HEADS_PER_STEP):
            cs = slice(head * hd, (head + 1) * hd)
            q = q_ref[rows, cs]
            k = k_ref[rows, cs]
            v = v_ref[rows, cs]
            s = lax.dot_general(q, k, (((1,), (1,)), ((), ())), preferred_element_type=F32) * dec_ref[head]
            intra = jnp.dot(s.astype(BF16), v, preferred_element_type=F32)
            state = st_ref[head]
            inter = jnp.dot(q, state.astype(BF16), preferred_element_type=F32) * qd_ref[head]
            kdec = (k.astype(F32) * kd_ref[head]).astype(BF16)
            kv = lax.dot_general(kdec, v, (((0,), (0,)), ((), ())), preferred_element_type=F32)
            st_ref[head] = state * cd_ref[head] + kv
            r = intra + inter
            mu = jnp.mean(r, axis=-1, keepdims=True)
            rc = r - mu
            var = jnp.mean(rc * rc, axis=-1, keepdims=True)
            rn = rc * lax.rsqrt(var + HEAD_NORM_EPS)
            gate = g_ref[rows, cs].astype(F32)
            o_ref[rows, cs] = (rn * nw_ref[:, cs] * (gate * jax.nn.sigmoid(gate))).astype(BF16)


def _retention_tables():
    C = RET_CHUNK
    f32 = np.float32
    lg = np.log(f32(1.0) - f32(2.0) ** (f32(-5.0) - np.arange(RET_HEADS, dtype=f32))).astype(f32)
    idx = np.arange(C, dtype=f32)
    diff = idx[:, None] - idx[None, :]
    intra = np.where(diff[None] >= 0, np.exp(np.maximum(diff, f32(0.0))[None] * lg[:, None, None]), f32(0.0))
    q_decay = np.exp((idx + f32(1.0))[None, :] * lg[:, None])
    k_decay = np.exp((f32(C - 1.0) - idx)[None, :] * lg[:, None])
    chunk_decay = np.exp(f32(C) * lg)
    bcast = (RET_HEADS, C, RET_HEAD_DIM)
    return (intra.astype(f32),
            np.ascontiguousarray(np.broadcast_to(q_decay[:, :, None], bcast)).astype(f32),
            np.ascontiguousarray(np.broadcast_to(k_decay[:, :, None], bcast)).astype(f32),
            np.ascontiguousarray(np.broadcast_to(chunk_decay[:, None, None],
                                                 (RET_HEADS, 1, RET_HEAD_DIM))).astype(f32))


def _retention(z, ret_norm_w, tables):
    T = z.shape[0]
    hd = RET_HEAD_DIM
    n_h = _RET_HEADS_PER_STEP
    width = n_h * hd
    per_tile = _PROJ_TN // width
    dec, qd, kd, cd = tables

    def zspec(col_tile):
        return pl.BlockSpec((SEQ, width), lambda b, h: (b, col_tile * per_tile + h))

    def tspec(shape):
        return pl.BlockSpec((n_h,) + shape, lambda b, h: (h, 0, 0))

    return pl.pallas_call(
        _retention_kernel,
        grid=(T // SEQ, RET_HEADS // n_h),
        in_specs=[zspec(_COL_RQ), zspec(_COL_RK), zspec(_COL_RV), zspec(_COL_RG),
                  pl.BlockSpec((1, width), lambda b, h: (0, h)),
                  tspec((RET_CHUNK, RET_CHUNK)), tspec((RET_CHUNK, hd)), tspec((RET_CHUNK, hd)),
                  tspec((1, hd))],
        out_specs=pl.BlockSpec((SEQ, width), lambda b, h: (b, h)),
        out_shape=jax.ShapeDtypeStruct((T, RET_HEADS * hd), BF16),
        scratch_shapes=[pltpu.VMEM((n_h, hd, hd), F32)],
        compiler_params=_cparams(("arbitrary", "arbitrary")),
        name="retention",
    )(z, z, z, z, ret_norm_w.reshape(1, RET_HEADS * hd), dec, qd, kd, cd)


_ATT_HEADS_PER_STEP = 2

def _attention_kernel(q0, k0, v0, q1, k1, v1, q2, k2, v2, g_ref, o_ref,
                      n2_ref, d2_ref, m2_ref, n12_ref, d12_ref, m12_ref, s_ref, p_ref, m_ref):
    blk = ATT_BLOCK
    hd = ATT_HEAD_DIM
    n_blk = SEQ // blk
    scale = hd ** -0.5
    neg_inf = jnp.float32(-jnp.inf)
    row = lax.broadcasted_iota(jnp.int32, (blk, 2 * blk), 0)
    col = lax.broadcasted_iota(jnp.int32, (blk, 2 * blk), 1)
    in_prev = col < blk
    valid = jnp.where(in_prev, -1, 1) * (row - jnp.where(in_prev, col, col - blk)) >= 0
    bias = jnp.where(valid, jnp.float32(0.0), neg_inf)
    contract_last = (((1,), (1,)), ((), ()))

    def group(d, q_ref, k_ref, v_ref, cs, finish):
        per_seq = n_blk // d
        for bi in range(n_blk):
            q = q_ref[bi * blk:(bi + 1) * blk, cs]
            if bi % per_seq == 0:
                s_ref[bi, :, :blk] = jnp.full((blk, blk), neg_inf, F32)
                s_ref[bi, :, blk:] = lax.dot_general(q, k_ref[bi * blk:(bi + 1) * blk, cs], contract_last,
                                                     preferred_element_type=F32)
            else:
                s_ref[bi] = lax.dot_general(q, k_ref[(bi - 1) * blk:(bi + 1) * blk, cs], contract_last,
                                            preferred_element_type=F32)
        s = s_ref[...] * scale + bias[None]
        m = jnp.max(s, axis=-1, keepdims=True)
        p_ref[...] = jnp.exp(s - m).astype(BF16)
        m_ref[...] = jnp.broadcast_to(m, (n_blk, blk, hd))
        for bi in range(n_blk):
            first = bi % per_seq == 0
            vv = v_ref[(bi if first else bi - 1) * blk:(bi + 1) * blk, cs]
            v_aug = jnp.concatenate([vv, jnp.ones_like(vv)], axis=1)
            pv = jnp.dot(p_ref[bi, :, blk:] if first else p_ref[bi], v_aug, preferred_element_type=F32)
            finish(bi, pv[:, :hd], pv[:, hd:], m_ref[bi])

    d1, d4, d16 = ATT_DILATIONS
    assert (d1, d4 * d4) == (1, d16)
    rows_by4 = SEQ // d4

    def finish_d16(bi, num, den, m):
        dst = pl.ds((bi % d4) * rows_by4 + bi // d4, blk, stride=d4)
        n2_ref[dst, :] = num
        d2_ref[dst, :] = den
        m2_ref[dst, :] = m

    def finish_d4(bi, num, den, m):
        rows = slice(bi * blk, (bi + 1) * blk)
        m2 = m2_ref[rows, :]
        m12 = jnp.maximum(m, m2)
        w1, w2 = jnp.exp(m - m12), jnp.exp(m2 - m12)
        r, i = divmod(bi, rows_by4 // blk)
        dst = pl.ds(i * blk * d4 + r, blk, stride=d4)
        n12_ref[dst, :] = w1 * num + w2 * n2_ref[rows, :]
        d12_ref[dst, :] = w1 * den + w2 * d2_ref[rows, :]
        m12_ref[dst, :] = m12

    def finish_d1(cs, bi, num, den, m):
        rows = slice(bi * blk, (bi + 1) * blk)
        m12 = m12_ref[rows, :]
        m_all = jnp.maximum(m, m12)
        w0, w12 = jnp.exp(m - m_all), jnp.exp(m12 - m_all)
        a = (w0 * num + w12 * n12_ref[rows, :]) / (w0 * den + w12 * d12_ref[rows, :])
        gate = g_ref[rows, cs].astype(F32)
        o_ref[rows, cs] = (a * (gate * jax.nn.sigmoid(gate))).astype(BF16)

    for head in range(_ATT_HEADS_PER_STEP):
        cs = slice(head * hd, (head + 1) * hd)
        group(d16, q2, k2, v2, cs, finish_d16)
        group(d4, q1, k1, v1, cs, finish_d4)
        group(d1, q0, k0, v0, cs, functools.partial(finish_d1, cs))


def _attention(z):
    T = z.shape[0]
    hd = ATT_HEAD_DIM
    width = _ATT_HEADS_PER_STEP * hd
    per_tile = _PROJ_TN // width

    def zspec(col_tile):
        return pl.BlockSpec((SEQ, width), lambda b, h: (b, col_tile * per_tile + h))

    in_specs = []
    for g in range(len(ATT_DILATIONS)):
        in_specs += [zspec(_COL_AQ + g), zspec(_COL_AK + g), zspec(_COL_AV + g)]
    in_specs.append(zspec(_COL_AG))
    return pl.pallas_call(
        _attention_kernel,
        grid=(T // SEQ, ATT_HEADS_PER_GROUP // _ATT_HEADS_PER_STEP),
        in_specs=in_specs,
        out_specs=pl.BlockSpec((SEQ, width), lambda b, h: (b, h)),
        out_shape=jax.ShapeDtypeStruct((T, ATT_HEADS_PER_GROUP * hd), BF16),
        scratch_shapes=[pltpu.VMEM((SEQ, hd), F32)] * 6 + [
                        pltpu.VMEM((SEQ // ATT_BLOCK, ATT_BLOCK, 2 * ATT_BLOCK), F32),
                        pltpu.VMEM((SEQ // ATT_BLOCK, ATT_BLOCK, 2 * ATT_BLOCK), BF16),
                        pltpu.VMEM((SEQ // ATT_BLOCK, ATT_BLOCK, hd), F32)],
        compiler_params=_cparams(("arbitrary", "arbitrary")),
        name="attention",
    )(*([z] * 10))


_CONV_ROWS = 512
_CONV_HALO = 32
_CONV_NORM_ROWS = 256


def _conv_kernel(cl_ref, cs_ref, g_ref, dw_ref, dwb_ref, lnw_ref, lnb_ref, o_ref, pad_ref, acc_ref):
    R, H = _CONV_ROWS, _CONV_HALO
    n_slab = D_MODEL // LANES

    @pl.when(pl.program_id(1) == 0)
    def _():
        pad_ref[:, 0:H, :] = jnp.zeros((n_slab, H, LANES), F32)

    lead = H - (CONV_TAPS - 1)
    for c in range(n_slab):
        cols = slice(c * LANES, (c + 1) * LANES)
        pad_ref[c, H:H + R, :] = cl_ref[:, cols].astype(F32) * cs_ref[:, cols].astype(F32)
    for c in range(n_slab):
        cols = slice(c * LANES, (c + 1) * LANES)
        acc = pad_ref[c, lead:lead + R, :] * dw_ref[0:1, cols]
        for k in range(1, CONV_TAPS):
            acc = acc + pad_ref[c, lead + k:lead + k + R, :] * dw_ref[k:k + 1, cols]
        acc_ref[:, cols] = acc
    for c in range(n_slab):
        pad_ref[c, 0:H, :] = pad_ref[c, R:R + H, :]

    def norm_rows(i, carry):
        rows = pl.ds(pl.multiple_of(i * _CONV_NORM_ROWS, _CONV_NORM_ROWS), _CONV_NORM_ROWS)
        c = acc_ref[rows, :] + dwb_ref[...]
        mu = jnp.mean(c, axis=-1, keepdims=True)
        cc = c - mu
        var = jnp.mean(cc * cc, axis=-1, keepdims=True)
        y = cc * lax.rsqrt(var + HEAD_NORM_EPS) * lnw_ref[...] + lnb_ref[...]
        y = y * jax.nn.sigmoid(y)
        o_ref[rows, :] = (y * g_ref[rows, :].astype(F32)).astype(BF16)
        return carry

    lax.fori_loop(0, R // _CONV_NORM_ROWS, norm_rows, 0)


def _conv(z, dw_w, dw_b, ln_w, ln_b):
    T = z.shape[0]
    R = _CONV_ROWS
    n_r = SEQ // R

    def zspec(col_tile):
        return pl.BlockSpec((R, D_MODEL), lambda b, j: (b * n_r + j, col_tile))

    def vec():
        return pl.BlockSpec((1, D_MODEL), lambda b, j: (0, 0))

    return pl.pallas_call(
        _conv_kernel,
        grid=(T // SEQ, n_r),
        in_specs=[zspec(_COL_CLIN), zspec(_COL_CGATE), zspec(_COL_CG),
                  pl.BlockSpec((CONV_TAPS, D_MODEL), lambda b, j: (0, 0)), vec(), vec(), vec()],
        out_specs=pl.BlockSpec((R, D_MODEL), lambda b, j: (b * n_r + j, 0)),
        out_shape=jax.ShapeDtypeStruct((T, D_MODEL), BF16),
        scratch_shapes=[pltpu.VMEM((D_MODEL // LANES, R + _CONV_HALO, LANES), F32),
                        pltpu.VMEM((R, D_MODEL), F32)],
        compiler_params=_cparams(("arbitrary", "arbitrary")),
        name="conv",
    )(z, z, z, dw_w, dw_b.reshape(1, D_MODEL), ln_w.reshape(1, D_MODEL), ln_b.reshape(1, D_MODEL))


_OUT_TM = 512


def _out_proj_kernel(final, r_ref, a_ref, c_ref, g0_ref, g1_ref, g2_ref, x_ref,
                     wr_ref, wa_ref, wc_ref, wo_ref, fw_ref, o_ref):
    def branch(in_ref, w_ref, gate_ref):
        y = jnp.dot(in_ref[...], w_ref[...], preferred_element_type=F32)
        return jax.nn.sigmoid(gate_ref[...].astype(F32)) * y

    merged = branch(r_ref, wr_ref, g0_ref) + branch(a_ref, wa_ref, g1_ref) + branch(c_ref, wc_ref, g2_ref)
    out = x_ref[...] + jnp.dot(merged.astype(BF16), wo_ref[...], preferred_element_type=F32)
    if final:
        out = out * lax.rsqrt(jnp.mean(out * out, axis=-1, keepdims=True) + NORM_EPS) * fw_ref[...]
    o_ref[...] = out


def _out_proj(rg, ag, cg, z, xf, w_r, w_a, w_c, w_o, final_w, final):
    T = xf.shape[0]
    tm = _OUT_TM

    def rows(col_tile=0):
        return pl.BlockSpec((tm, D_MODEL), lambda i: (i, col_tile))

    def weight():
        return pl.BlockSpec((D_MODEL, D_MODEL), lambda i: (0, 0))

    return pl.pallas_call(
        functools.partial(_out_proj_kernel, final),
        grid=(T // tm,),
        in_specs=[rows(), rows(), rows(), rows(_COL_MG), rows(_COL_MG + 1), rows(_COL_MG + 2), rows(),
                  weight(), weight(), weight(), weight(),
                  pl.BlockSpec((1, D_MODEL), lambda i: (0, 0))],
        out_specs=rows(),
        out_shape=jax.ShapeDtypeStruct((T, D_MODEL), F32),
        compiler_params=_cparams(("arbitrary",)),
        name="out_proj_final" if final else "out_proj",
    )(rg, ag, cg, z, z, z, xf, w_r, w_a, w_c, w_o, final_w.reshape(1, D_MODEL))


def kernel(x, norm_w, w_in, b_in, ret_norm_w, ret_w_o, att_w_o, conv_dw_w, conv_dw_b,
           conv_norm_w, conv_norm_b, conv_w_o, w_out, final_norm_w):
    B, S, D = x.shape
    assert (S, D) == (SEQ, D_MODEL) and w_in.shape[-1] == IN_WIDTH
    depth = w_in.shape[0]
    t256, t128 = _rope_tables()
    ret_tables = _retention_tables()
    xf = x.reshape(B * S, D)
    for l in range(depth):
        hs = _norm_perm(xf, norm_w[l])
        z = _in_proj(hs, w_in, l, b_in[l], t256, t128)
        rg = _retention(z, ret_norm_w[l], ret_tables)
        ag = _attention(z)
        cg = _conv(z, conv_dw_w[l], conv_dw_b[l], conv_norm_w[l], conv_norm_b[l])
        xf = _out_proj(rg, ag, cg, z, xf,
                       ret_w_o[l].astype(BF16), att_w_o[l].astype(BF16), conv_w_o[l].astype(BF16),
                       w_out[l].astype(BF16), final_norm_w, final=(l == depth - 1))
    return xf.reshape(B, S, D)
```
